```python
import jax, jax.numpy as jnp
from jax import lax
import numpy as np

D_MODEL = 1024
BATCH = 8
SEQ = 4096
DEPTH = 2

N_MEM = 256
RWKV_HEAD = 64
RWKV_WIDTH = D_MODEL
RWKV_HEADS = RWKV_WIDTH // RWKV_HEAD
DECAY_LORA = 64
AAA_LORA = 64
VRES_LORA = 32
RWKV_GN_EPS = 64e-5
CONV_WIDTH = D_MODEL
CONV_KERNEL = 31
LN_EPS = 1e-5
MEM_HEADS = 4
MEM_WIDTH = D_MODEL
MEM_HEAD_DIM = MEM_WIDTH // MEM_HEADS
N_BRANCH = 3
RMS_EPS = 1e-6

SHIFT_SIZES = (RWKV_WIDTH, RWKV_WIDTH, RWKV_WIDTH, DECAY_LORA, AAA_LORA)
REST_SIZES = (RWKV_WIDTH, 2 * CONV_WIDTH, CONV_WIDTH, MEM_WIDTH, MEM_WIDTH, N_BRANCH * D_MODEL)
RWKV_SHIFT = sum(SHIFT_SIZES)
N_IN = RWKV_SHIFT + sum(REST_SIZES)

kernel_name = "hybrid_rwkv7_conformer_memxattn_gated"


def _split(a, sizes):
    pts = [int(v) for v in np.cumsum(sizes)[:-1]]
    return jnp.split(a, pts, axis=-1)


def rms_norm(x, g):
    xf = x.astype(jnp.float32)
    y = xf * lax.rsqrt(jnp.mean(xf * xf, axis=-1, keepdims=True) + RMS_EPS)
    return (y * g.astype(jnp.float32)).astype(x.dtype)


def token_shift_mix(p, mu):
    prev = jnp.pad(p, ((0, 0), (1, 0), (0, 0)))[:, :-1]
    return p + (prev - p) * mu


def wkv7_scan(r, decay, k, v, a_vec, b_vec):
    B, S, H, N = r.shape
    seqs = tuple(jnp.moveaxis(t.astype(jnp.float32), 1, 0) for t in (r, decay, k, v, a_vec, b_vec))

    def step(state, inp):
        r_t, w_t, k_t, v_t, a_t, b_t = inp
        sa = jnp.einsum('bhvk,bhk->bhv', state, a_t)
        state = (state * w_t[:, :, None, :]
                 + sa[..., :, None] * b_t[:, :, None, :]
                 + v_t[..., :, None] * k_t[:, :, None, :])
        y = jnp.einsum('bhvk,bhk->bhv', state, r_t)
        return state, y

    s0 = jnp.zeros((B, H, N, N), jnp.float32)
    _, ys = lax.scan(step, s0, seqs)
    return jnp.moveaxis(ys, 0, 1)


def hybrid_layer(x, mem, v_first, vres, g_norm, w_in, mu_shift, w0, w_decay_up, a0, w_aaa_up,
                 k_k, k_a, r_k, gn_g, gn_b, w_proj_rwkv, b_glu, w_dw, b_dw, ln_g, ln_b,
                 w_proj_conv, b_proj_conv, g_mem_norm, w_mem_kv, w_proj_mem, w_out):
    B, S, _ = x.shape
    H, N = RWKV_HEADS, RWKV_HEAD
    h = rms_norm(x, g_norm)
    w_all = w_in if vres is None else jnp.concatenate([w_in, vres[0]], axis=1)
    proj = h @ w_all

    shifted = token_shift_mix(proj[..., :RWKV_SHIFT], mu_shift)
    r, k, v, w_lo, a_lo = _split(shifted, SHIFT_SIZES)
    rwkv_gate, glu_in, conv_gate, q, mem_gate, merge_logits = _split(proj[..., RWKV_SHIFT:N_IN], REST_SIZES)

    decay_log = -jax.nn.softplus(-(w0 + jnp.tanh(w_lo) @ w_decay_up)) - 0.5
    decay = jnp.exp(-jnp.exp(decay_log.astype(jnp.float32)))
    a = jax.nn.sigmoid(a0 + a_lo @ w_aaa_up)
    if vres is None:
        v_first = v
    else:
        _, mu_vres, v0, w_vres_up = vres
        v_lo = token_shift_mix(proj[..., N_IN:], mu_vres)
        v = v + (v_first - v) * jax.nn.sigmoid(v0 + v_lo @ w_vres_up)
    kk = (k * k_k).reshape(B, S, H, N).astype(jnp.float32)
    kk = kk / jnp.maximum(jnp.sqrt(jnp.sum(kk * kk, axis=-1, keepdims=True)), 1e-12)
    k = k * (1.0 + (a - 1.0) * k_a)
    rh = r.reshape(B, S, H, N)
    kh = k.reshape(B, S, H, N)
    vh = v.reshape(B, S, H, N)
    ah = a.reshape(B, S, H, N).astype(jnp.float32)
    wkv = wkv7_scan(rh, decay.reshape(B, S, H, N), kh, vh, -kk, kk * ah)
    mu = jnp.mean(wkv, axis=-1, keepdims=True)
    var = jnp.mean(jnp.square(wkv - mu), axis=-1, keepdims=True)
    wkv = ((wkv - mu) * lax.rsqrt(var + RWKV_GN_EPS)).reshape(B, S, RWKV_WIDTH)
    wkv = (wkv * gn_g.astype(jnp.float32) + gn_b.astype(jnp.float32)).astype(x.dtype)
    bonus = jnp.sum(rh * kh * r_k, axis=-1, keepdims=True) * vh
    o_rwkv = wkv + bonus.reshape(B, S, RWKV_WIDTH)
    y_rwkv = (o_rwkv * jax.nn.silu(rwkv_gate)) @ w_proj_rwkv

    glu_in = glu_in + b_glu
    u = glu_in[..., :CONV_WIDTH] * jax.nn.sigmoid(glu_in[..., CONV_WIDTH:])
    u = lax.conv_general_dilated(u, w_dw[:, None, :], window_strides=(1,),
                                 padding=[(CONV_KERNEL - 1, 0)],
                                 dimension_numbers=('NWC', 'WIO', 'NWC'),
                                 feature_group_count=CONV_WIDTH) + b_dw
    uf = u.astype(jnp.float32)
    um = jnp.mean(uf, axis=-1, keepdims=True)
    uv = jnp.mean(jnp.square(uf - um), axis=-1, keepdims=True)
    u = ((uf - um) * lax.rsqrt(uv + LN_EPS) * ln_g.astype(jnp.float32) + ln_b.astype(jnp.float32)).astype(x.dtype)
    u = jax.nn.silu(u) * jax.nn.silu(conv_gate)
    y_conv = u @ w_proj_conv + b_proj_conv

    m = rms_norm(mem, g_mem_norm)
    km, vm = _split(m @ w_mem_kv, (MEM_WIDTH, MEM_WIDTH))
    qh = q.reshape(B, S, MEM_HEADS, MEM_HEAD_DIM)
    kmh = km.reshape(B, -1, MEM_HEADS, MEM_HEAD_DIM)
    vmh = vm.reshape(B, -1, MEM_HEADS, MEM_HEAD_DIM)
    scores = jnp.einsum('bshd,bmhd->bhsm', qh, kmh).astype(jnp.float32) * (MEM_HEAD_DIM ** -0.5)
    probs = jax.nn.softmax(scores, axis=-1).astype(x.dtype)
    att = jnp.einsum('bhsm,bmhd->bshd', probs, vmh).reshape(B, S, MEM_WIDTH)
    y_mem = (att * jax.nn.silu(mem_gate)) @ w_proj_mem

    gates = jax.nn.sigmoid(merge_logits).reshape(B, S, N_BRANCH, D_MODEL)
    y = gates[:, :, 0] * y_rwkv + gates[:, :, 1] * y_conv + gates[:, :, 2] * y_mem
    return x + y @ w_out, v_first


def setup_inputs(seed: int = 0) -> dict:
    key = jax.random.key(seed)
    ks = jax.random.split(key, 32)
    f = jnp.float32
    D, W, C, L = D_MODEL, RWKV_WIDTH, CONV_WIDTH, DEPTH
    nrm = lambda k, shape, s: jax.random.normal(k, shape, f) * s
    return {
        "x": jax.random.normal(ks[0], (BATCH, SEQ, D), f),
        "mem": jax.random.normal(ks[1], (BATCH, N_MEM, D), f),
        "g_norm": 1.0 + nrm(ks[2], (L, D), 0.02),
        "w_in": nrm(ks[3], (L, D, N_IN), D ** -0.5),
        "mu_shift": jax.random.uniform(ks[4], (L, RWKV_SHIFT), f),
        "w0": jax.random.uniform(ks[5], (L, W), f, -5.0, 0.0),
        "w_decay_up": nrm(ks[6], (L, DECAY_LORA, W), 0.5 * DECAY_LORA ** -0.5),
        "a0": nrm(ks[7], (L, W), 0.1),
        "w_aaa_up": nrm(ks[8], (L, AAA_LORA, W), 0.5 * AAA_LORA ** -0.5),
        "k_k": 0.85 + nrm(ks[9], (L, W), 0.02),
        "k_a": 1.0 + nrm(ks[10], (L, W), 0.02),
        "r_k": nrm(ks[11], (L, RWKV_HEADS, RWKV_HEAD), 0.1),
        "gn_g": 1.0 + nrm(ks[12], (L, W), 0.02),
        "gn_b": nrm(ks[13], (L, W), 0.01),
        "w_proj_rwkv": nrm(ks[14], (L, W, D), W ** -0.5),
        "w_vres_down": nrm(ks[15], (L - 1, D, VRES_LORA), D ** -0.5),
        "mu_vres": jax.random.uniform(ks[16], (L - 1, VRES_LORA), f),
        "v0": nrm(ks[17], (L - 1, W), 0.1),
        "w_vres_up": nrm(ks[18], (L - 1, VRES_LORA, W), 0.5 * VRES_LORA ** -0.5),
        "b_glu": nrm(ks[19], (L, 2 * C), 0.01),
        "w_dw": nrm(ks[20], (L, CONV_KERNEL, C), CONV_KERNEL ** -0.5),
        "b_dw": nrm(ks[21], (L, C), 0.01),
        "ln_g": 1.0 + nrm(ks[22], (L, C), 0.02),
        "ln_b": nrm(ks[23], (L, C), 0.01),
        "w_proj_conv": nrm(ks[24], (L, C, D), C ** -0.5),
        "b_proj_conv": nrm(ks[25], (L, D), 0.01),
        "g_mem_norm": 1.0 + nrm(ks[26], (L, D), 0.02),
        "w_mem_kv": nrm(ks[27], (L, D, 2 * MEM_WIDTH), D ** -0.5),
        "w_proj_mem": nrm(ks[28], (L, MEM_WIDTH, D), MEM_WIDTH ** -0.5),
        "w_out": nrm(ks[29], (L, D, D), D ** -0.5),
        "g_final": 1.0 + nrm(ks[30], (D,), 0.02),
    }


def reference(x, mem, g_norm, w_in, mu_shift, w0, w_decay_up, a0, w_aaa_up, k_k, k_a, r_k,
              gn_g, gn_b, w_proj_rwkv, w_vres_down, mu_vres, v0, w_vres_up, b_glu, w_dw, b_dw,
              ln_g, ln_b, w_proj_conv, b_proj_conv, g_mem_norm, w_mem_kv, w_proj_mem, w_out,
              g_final):
    v_first = None
    for i in range(DEPTH):
        vres = None if i == 0 else (w_vres_down[i - 1], mu_vres[i - 1], v0[i - 1], w_vres_up[i - 1])
        x, v_first = hybrid_layer(
            x, mem, v_first, vres, g_norm[i], w_in[i], mu_shift[i], w0[i], w_decay_up[i], a0[i],
            w_aaa_up[i], k_k[i], k_a[i], r_k[i], gn_g[i], gn_b[i], w_proj_rwkv[i], b_glu[i],
            w_dw[i], b_dw[i], ln_g[i], ln_b[i], w_proj_conv[i], b_proj_conv[i], g_mem_norm[i],
            w_mem_kv[i], w_proj_mem[i], w_out[i])
    return rms_norm(x, g_final)
```

```python
import functools
import math

import numpy as np
import jax
import jax.numpy as jnp
from jax import lax
from jax.experimental import pallas as pl
from jax.experimental.pallas import tpu as pltpu

F32 = jnp.float32
BF16 = jnp.bfloat16

D_MODEL = 1024
N_MEM = 256
HEAD = 64
PAIR = 2 * HEAD
N_PAIR = D_MODEL // PAIR
CHUNK = 64
LORA_IN = 64
VRES_IN = 32
CONV_K = 31
CONV_HALO = 32
MEM_HEADS = 4
MEM_HEAD_DIM = D_MODEL // MEM_HEADS
RMS_EPS = 1e-6
LN_EPS = 1e-5
GN_EPS = 64e-5
SEG = 256
TOKEN_BLOCK = 256
VMEM_LIMIT = 56 * 1024 * 1024
INVERSE_LEVELS = (2, 4, 8, 16, 32)
M_STRICT, M_INCL, M_SAME, M_EYE, M_BLK2, M_OFF0 = 0, 1, 2, 3, 4, 5


def _mm(a, b):
    return jnp.dot(a.astype(BF16), b.astype(BF16), preferred_element_type=F32)


def _mm_nt(a, b):
    return lax.dot_general(a.astype(BF16), b.astype(BF16), (((1,), (1,)), ((), ())),
                           preferred_element_type=F32)


def _mm_tn(a, b):
    return lax.dot_general(a.astype(BF16), b.astype(BF16), (((0,), (0,)), ((), ())),
                           preferred_element_type=F32)


def _rms_norm(x, g):
    return x * lax.rsqrt(jnp.mean(x * x, axis=-1, keepdims=True) + RMS_EPS) * g


def _sigmoid(x):
    return 1.0 / (1.0 + jnp.exp(-x))


def _silu(x):
    return x * _sigmoid(x)


def _split3(x):
    hi = x.astype(BF16)
    r1 = x - hi.astype(F32)
    mid = r1.astype(BF16)
    lo = (r1 - mid.astype(F32)).astype(BF16)
    return hi, mid, lo


def _seg_sum(x, ones_bd):
    hi = x.astype(BF16)
    lo = (x - hi.astype(F32)).astype(BF16)
    outs = []
    for g in range(D_MODEL // SEG):
        sl = slice(g * SEG, (g + 1) * SEG)
        outs.append(jnp.dot(hi[:, sl], ones_bd, preferred_element_type=F32)
                    + jnp.dot(lo[:, sl], ones_bd, preferred_element_type=F32))
    return jnp.concatenate(outs, axis=1)


def _rwkv_kernel(has_vres, tb, *refs):
    if has_vres:
        (x_ref, vf_ref, g_ref, wa_ref, mu_ref, wl_ref, w0_ref, a0_ref, v0_ref, kk_ref, ka_ref, rk_ref,
         gng_ref, gnb_ref, wp_ref, cm_ref, tri_ref, ones_ref,
         y_ref,
         carry_ref, state_ref, at_ref, rt_ref, bt_ref, kt_ref, bh_ref, kh_ref, vv_ref, pc_ref, wkv_ref) = refs
        vout_ref = None
    else:
        (x_ref, g_ref, wa_ref, mu_ref, wl_ref, w0_ref, a0_ref, kk_ref, ka_ref, rk_ref,
         gng_ref, gnb_ref, wp_ref, cm_ref, tri_ref, ones_ref,
         y_ref, vout_ref,
         carry_ref, state_ref, at_ref, rt_ref, bt_ref, kt_ref, bh_ref, kh_ref, vv_ref, pc_ref, wkv_ref) = refs
        vf_ref = v0_ref = None
    lw_in = wl_ref.shape[0]
    sh = 3 * D_MODEL + lw_in

    @pl.when(pl.program_id(1) == 0)
    def _():
        carry_ref[...] = jnp.zeros_like(carry_ref)
        state_ref[...] = jnp.zeros_like(state_ref)

    x = x_ref[...]
    h = _rms_norm(x, g_ref[...]).astype(BF16)
    ps = jnp.dot(h, wa_ref[:, :sh], preferred_element_type=F32)
    gate = jnp.dot(h, wa_ref[:, sh:], preferred_element_type=F32)

    row = lax.broadcasted_iota(jnp.int32, (tb, 1), 0)
    prev = jnp.where(row == 0, carry_ref[0:1, :], pltpu.roll(ps, 1, 0))
    carry_ref[0:1, :] = ps[tb - 1:tb, :]
    shifted = ps + (prev - ps) * mu_ref[...]
    r = shifted[:, 0:D_MODEL]
    k = shifted[:, D_MODEL:2 * D_MODEL]
    v = shifted[:, 2 * D_MODEL:3 * D_MODEL]
    lo_in = shifted[:, 3 * D_MODEL:sh]
    lane = lax.broadcasted_iota(jnp.int32, lo_in.shape, 1)
    lo_act = jnp.where(lane < LORA_IN, jnp.tanh(lo_in), lo_in)
    lora = _mm(lo_act, wl_ref[...])

    logw = -math.exp(-0.5) * _sigmoid(w0_ref[...] + lora[:, 0:D_MODEL])
    a_icl = _sigmoid(a0_ref[...] + lora[:, D_MODEL:2 * D_MODEL])
    if has_vres:
        v = v + (vf_ref[...] - v) * _sigmoid(v0_ref[...] + lora[:, 2 * D_MODEL:3 * D_MODEL])
    else:
        vout_ref[...] = v
    ones_bd = ones_ref[...]
    kk = k * kk_ref[...]
    kk = kk / jnp.maximum(jnp.sqrt(_seg_sum(kk * kk, ones_bd)), 1e-12)
    k2 = k * (1.0 + (a_icl - 1.0) * ka_ref[...])
    b_vec = kk * a_icl

    tri = tri_ref[...]
    hi, mid, lo = _split3(logw)
    cum_tot = (jnp.dot(tri, hi, preferred_element_type=F32) + jnp.dot(tri, mid, preferred_element_type=F32)
               + jnp.dot(tri, lo, preferred_element_type=F32))
    cum = cum_tot[0:tb]
    tot = cum_tot[tb:2 * tb]
    e_k = jnp.exp(-cum)
    e_h = jnp.exp(tot - cum)
    at_ref[...] = (-kk * jnp.exp(cum - logw)).astype(BF16)
    rt_ref[...] = (r * jnp.exp(cum)).astype(BF16)
    bt_ref[...] = (b_vec * e_k).astype(BF16)
    kt_ref[...] = (k2 * e_k).astype(BF16)
    bh_ref[...] = (b_vec * e_h).astype(BF16)
    kh_ref[...] = (k2 * e_h).astype(BF16)
    vv_ref[...] = v.astype(BF16)
    pc_ref[...] = jnp.exp(tot)

    lane_p = lax.broadcasted_iota(jnp.int32, (CHUNK, PAIR), 1)
    is_even = lane_p < HEAD

    def dup_mask(t):
        zero = jnp.zeros_like(t)
        return jnp.concatenate([jnp.where(is_even, t, zero), jnp.where(is_even, zero, t)], axis=0)

    m_strict = cm_ref[M_STRICT]
    m_incl = cm_ref[M_INCL]
    m_same = cm_ref[M_SAME]

    def chunk_body(c, carry):
        rows = pl.ds(pl.multiple_of(c * CHUNK, CHUNK), CHUNK)
        for p in range(N_PAIR):
            lanes = slice(p * PAIR, (p + 1) * PAIR)
            lhs = jnp.concatenate([dup_mask(at_ref[rows, lanes]), dup_mask(rt_ref[rows, lanes])], axis=0)
            bt = bt_ref[rows, lanes]
            kt = kt_ref[rows, lanes]
            gmat = _mm_nt(lhs, jnp.concatenate([bt, bt, kt, kt], axis=0))
            xab = gmat[0:PAIR, 0:PAIR] * m_strict
            aak = gmat[0:PAIR, PAIR:2 * PAIR] * m_strict
            arb = gmat[PAIR:2 * PAIR, 0:PAIR] * m_incl
            ark = gmat[PAIR:2 * PAIR, PAIR:2 * PAIR] * m_incl
            tinv = cm_ref[M_EYE] + xab * cm_ref[M_BLK2]
            for lvl in range(len(INVERSE_LEVELS)):
                tb16 = tinv.astype(BF16)
                tinv = tinv + _mm(_mm(tb16, xab * cm_ref[M_OFF0 + lvl]), tb16)
            s0 = state_ref[p]
            ar = _mm_nt(lhs, s0)
            vbd = dup_mask(vv_ref[rows, lanes])
            u = _mm(tinv, ar[0:PAIR] + _mm(aak, vbd)) * m_same
            uv = jnp.concatenate([u.astype(BF16), vbd], axis=0)
            y = ar[PAIR:2 * PAIR] + _mm(jnp.concatenate([arb, ark], axis=1), uv)
            wkv_ref[rows, lanes] = jnp.where(is_even, y[0:CHUNK], y[CHUNK:PAIR])
            bk = jnp.concatenate([dup_mask(bh_ref[rows, lanes]), dup_mask(kh_ref[rows, lanes])], axis=0)
            pc = pc_ref[pl.ds(pl.multiple_of(c * CHUNK, CHUNK), 1), lanes]
            state_ref[p] = s0 * pc + _mm_tn(uv, bk)
        return carry

    lax.fori_loop(0, tb // CHUNK, chunk_body, 0)

    wkv = wkv_ref[...]
    inv_n = 1.0 / HEAD
    mean = _seg_sum(wkv, ones_bd) * inv_n
    dev = wkv - mean
    var = _seg_sum(dev * dev, ones_bd) * inv_n
    wkv_n = dev * lax.rsqrt(var + GN_EPS) * gng_ref[...] + gnb_ref[...]
    bonus = _seg_sum(r * k2 * rk_ref[...], ones_bd) * v
    y_ref[...] = _mm((wkv_n + bonus) * _silu(gate), wp_ref[...])


def _const_spec(shape):
    nd = len(shape)
    return pl.BlockSpec(shape, lambda b, s: (0,) * nd, pipeline_mode=pl.Buffered(1))


def _rwkv_call(x, v_first, g, wa, mu, wl, w0, a0, v0, k_k, k_a, r_k, gn_g, gn_b, wp, cm, tri, ones_bd):
    bsz, seq, d = x.shape
    tb = min(TOKEN_BLOCK, seq)
    has_vres = v_first is not None
    blk = pl.BlockSpec((None, tb, d), lambda b, s: (b, s, 0))
    sh = mu.shape[1]
    consts = [g, wa, mu, wl, w0, a0] + ([v0] if has_vres else []) + [k_k, k_a, r_k, gn_g, gn_b, wp, cm, tri, ones_bd]
    in_specs = [blk] + ([blk] if has_vres else []) + [_const_spec(c.shape) for c in consts]
    args = [x] + ([v_first] if has_vres else []) + consts
    out_sds = jax.ShapeDtypeStruct((bsz, seq, d), F32)
    big = lambda dt: pltpu.VMEM((tb, d), dt)
    scratch = [pltpu.VMEM((8, sh), F32), pltpu.VMEM((N_PAIR, PAIR, PAIR), F32),
               big(BF16), big(BF16), big(BF16), big(BF16), big(BF16), big(BF16), big(BF16), big(F32), big(F32)]
    return pl.pallas_call(
        functools.partial(_rwkv_kernel, has_vres, tb),
        grid=(bsz, seq // tb),
        in_specs=in_specs,
        out_specs=blk if has_vres else (blk, blk),
        out_shape=out_sds if has_vres else (out_sds, out_sds),
        scratch_shapes=scratch,
        compiler_params=pltpu.CompilerParams(dimension_semantics=("arbitrary", "arbitrary"),
                                             vmem_limit_bytes=VMEM_LIMIT),
        name="rwkv_vres" if has_vres else "rwkv",
    )(*args)


def _token_kernel(final_norm, tb, *refs):
    if final_norm:
        (x_ref, yr_ref, kv_ref, g_ref, wb_ref, bglu_ref, wdw_ref, bdw_ref, lng_ref, lnb_ref, wpc_ref, bpc_ref,
         wpm_ref, wout_ref, gfin_ref, out_ref, ubuf_ref) = refs
    else:
        (x_ref, yr_ref, kv_ref, g_ref, wb_ref, bglu_ref, wdw_ref, bdw_ref, lng_ref, lnb_ref, wpc_ref, bpc_ref,
         wpm_ref, wout_ref, out_ref, ubuf_ref) = refs
        gfin_ref = None
    d = D_MODEL

    @pl.when(pl.program_id(1) == 0)
    def _():
        ubuf_ref[0:CONV_HALO, :] = jnp.zeros((CONV_HALO, d), F32)

    x = x_ref[...]
    h = _rms_norm(x, g_ref[...]).astype(BF16)

    glu = jnp.dot(h, wb_ref[:, 0:2 * d], preferred_element_type=F32) + bglu_ref[...]
    ubuf_ref[CONV_HALO:CONV_HALO + tb, :] = glu[:, 0:d] * _sigmoid(glu[:, d:2 * d])
    acc = jnp.zeros((tb, d), F32) + bdw_ref[...]
    for j in range(CONV_K):
        off = CONV_HALO - (CONV_K - 1) + j
        acc = acc + ubuf_ref[off:off + tb, :] * wdw_ref[j:j + 1, :]
    ubuf_ref[0:CONV_HALO, :] = ubuf_ref[tb:tb + CONV_HALO, :]
    um = jnp.mean(acc, axis=-1, keepdims=True)
    ud = acc - um
    uv = jnp.mean(ud * ud, axis=-1, keepdims=True)
    u = ud * lax.rsqrt(uv + LN_EPS) * lng_ref[...] + lnb_ref[...]
    cgate = jnp.dot(h, wb_ref[:, 2 * d:3 * d], preferred_element_type=F32)
    y_conv = _mm(_silu(u) * _silu(cgate), wpc_ref[...]) + bpc_ref[...]

    q = jnp.dot(h, wb_ref[:, 3 * d:4 * d], preferred_element_type=F32).astype(BF16)
    heads = []
    for hd in range(MEM_HEADS):
        sl = slice(hd * MEM_HEAD_DIM, (hd + 1) * MEM_HEAD_DIM)
        sc = lax.dot_general(q[:, sl], kv_ref[:, sl], (((1,), (1,)), ((), ())),
                             preferred_element_type=F32) * (MEM_HEAD_DIM ** -0.5)
        sc = jnp.exp(sc - jnp.max(sc, axis=-1, keepdims=True))
        prob = sc / jnp.sum(sc, axis=-1, keepdims=True)
        heads.append(jnp.dot(prob.astype(BF16), kv_ref[:, d + hd * MEM_HEAD_DIM:d + (hd + 1) * MEM_HEAD_DIM],
                             preferred_element_type=F32))
    att = jnp.concatenate(heads, axis=1)
    mgate = jnp.dot(h, wb_ref[:, 4 * d:5 * d], preferred_element_type=F32)
    y_mem = _mm(att * _silu(mgate), wpm_ref[...])

    y = _sigmoid(jnp.dot(h, wb_ref[:, 5 * d:6 * d], preferred_element_type=F32)) * yr_ref[...]
    y = y + _sigmoid(jnp.dot(h, wb_ref[:, 6 * d:7 * d], preferred_element_type=F32)) * y_conv
    y = y + _sigmoid(jnp.dot(h, wb_ref[:, 7 * d:8 * d], preferred_element_type=F32)) * y_mem
    out = x + _mm(y, wout_ref[...])
    if final_norm:
        out = _rms_norm(out, gfin_ref[...])
    out_ref[...] = out


def _token_call(x, y_rwkv, kv, g, wb, b_glu, w_dw, b_dw, ln_g, ln_b, wpc, bpc, wpm, wout, g_final):
    bsz, seq, d = x.shape
    tb = min(TOKEN_BLOCK, seq)
    final_norm = g_final is not None
    blk = pl.BlockSpec((None, tb, d), lambda b, s: (b, s, 0))
    kv_spec = pl.BlockSpec((None, N_MEM, 2 * d), lambda b, s: (b, 0, 0))
    consts = [g, wb, b_glu, w_dw, b_dw, ln_g, ln_b, wpc, bpc, wpm, wout] + ([g_final] if final_norm else [])
    return pl.pallas_call(
        functools.partial(_token_kernel, final_norm, tb),
        grid=(bsz, seq // tb),
        in_specs=[blk, blk, kv_spec] + [_const_spec(c.shape) for c in consts],
        out_specs=blk,
        out_shape=jax.ShapeDtypeStruct((bsz, seq, d), F32),
        scratch_shapes=[pltpu.VMEM((tb + CONV_HALO, d), F32)],
        compiler_params=pltpu.CompilerParams(dimension_semantics=("arbitrary", "arbitrary"),
                                             vmem_limit_bytes=VMEM_LIMIT),
        name="token_final" if final_norm else "token",
    )(x, y_rwkv, kv, *consts)


def _memkv_kernel(mem_ref, g_ref, w_ref, kv_ref):
    m = _rms_norm(mem_ref[...], g_ref[...])
    kv_ref[...] = _mm(m, w_ref[...]).astype(BF16)


def _memkv_call(mem, g_mem, w_kv):
    bsz, n_mem, d = mem.shape
    depth = g_mem.shape[0]
    return pl.pallas_call(
        _memkv_kernel,
        grid=(depth, bsz),
        in_specs=[pl.BlockSpec((None, n_mem, d), lambda l, b: (b, 0, 0)),
                  pl.BlockSpec((None, 1, d), lambda l, b: (l, 0, 0)),
                  pl.BlockSpec((None, d, 2 * d), lambda l, b: (l, 0, 0))],
        out_specs=pl.BlockSpec((None, None, n_mem, 2 * d), lambda l, b: (l, b, 0, 0)),
        out_shape=jax.ShapeDtypeStruct((depth, bsz, n_mem, 2 * d), BF16),
        compiler_params=pltpu.CompilerParams(dimension_semantics=("arbitrary", "arbitrary"),
                                             vmem_limit_bytes=VMEM_LIMIT),
        name="mem_kv",
    )(mem, g_mem, w_kv)


def _pair_masks():
    i = np.arange(PAIR)
    row, col = i[:, None], i[None, :]
    same = (row // HEAD) == (col // HEAD)
    rl, cl = row % HEAD, col % HEAD
    masks = [same & (cl < rl), same & (cl <= rl), same, row == col, same & (rl // 2 == cl // 2) & (cl < rl)]
    for n in INVERSE_LEVELS:
        masks.append(same & (rl // (2 * n) == cl // (2 * n)) & ((rl // n) % 2 == 1) & ((cl // n) % 2 == 0))
    return jnp.asarray(np.stack(masks).astype(np.float32))


def _cumsum_matrix(tb):
    i = np.arange(tb)
    same = (i[:, None] // CHUNK) == (i[None, :] // CHUNK)
    tri = same & (i[None, :] <= i[:, None])
    return jnp.asarray(np.concatenate([tri, same], axis=0).astype(np.float32)).astype(BF16)


def _segment_ones():
    i = np.arange(SEG)
    return jnp.asarray(((i[:, None] // HEAD) == (i[None, :] // HEAD)).astype(np.float32)).astype(BF16)


def kernel(x, mem, g_norm, w_in, mu_shift, w0, w_decay_up, a0, w_aaa_up, k_k, k_a, r_k, gn_g, gn_b, w_proj_rwkv, w_vres_down, mu_vres, v0, w_vres_up, b_glu, w_dw, b_dw, ln_g, ln_b, w_proj_conv, b_proj_conv, g_mem_norm, w_mem_kv, w_proj_mem, w_out, g_final):
    depth, d = g_norm.shape
    seq = x.shape[1]
    tb = min(TOKEN_BLOCK, seq)
    n_shift = 3 * d + 2 * LORA_IN
    row = lambda a: a.reshape(1, -1).astype(F32)
    cm, tri, ones_bd = _pair_masks(), _cumsum_matrix(tb), _segment_ones()
    kv = _memkv_call(mem, g_mem_norm.reshape(depth, 1, d), w_mem_kv.astype(BF16))

    v_first = None
    for i in range(depth):
        has_vres = i > 0
        lw_in = 2 * PAIR if has_vres else PAIR
        w_shift, w_rest = w_in[i][:, :n_shift], w_in[i][:, n_shift:]
        wl = jnp.zeros((lw_in, (3 if has_vres else 2) * d), F32)
        wl = wl.at[0:LORA_IN, 0:d].set(w_decay_up[i]).at[LORA_IN:2 * LORA_IN, d:2 * d].set(w_aaa_up[i])
        if has_vres:
            pad_w = jnp.zeros((d, PAIR - VRES_IN), F32)
            wa = jnp.concatenate([w_shift, w_vres_down[i - 1], pad_w, w_rest[:, :d]], axis=1)
            mu = jnp.concatenate([mu_shift[i], mu_vres[i - 1], jnp.zeros((PAIR - VRES_IN,), F32)])
            wl = wl.at[PAIR:PAIR + VRES_IN, 2 * d:3 * d].set(w_vres_up[i - 1])
        else:
            wa = jnp.concatenate([w_shift, w_rest[:, :d]], axis=1)
            mu = mu_shift[i]
        outs = _rwkv_call(x, v_first, row(g_norm[i]), wa.astype(BF16), row(mu), wl.astype(BF16), row(w0[i]),
                          row(a0[i]), row(v0[i - 1]) if has_vres else None, row(k_k[i]), row(k_a[i]), row(r_k[i]),
                          row(gn_g[i]), row(gn_b[i]), w_proj_rwkv[i].astype(BF16), cm, tri, ones_bd)
        if has_vres:
            y_rwkv = outs
        else:
            y_rwkv, v_first = outs
        w_dw_pad = jnp.concatenate([w_dw[i], jnp.zeros((CONV_HALO - CONV_K, d), F32)], axis=0)
        x = _token_call(x, y_rwkv, kv[i], row(g_norm[i]), w_rest[:, d:].astype(BF16), row(b_glu[i]), w_dw_pad,
                        row(b_dw[i]), row(ln_g[i]), row(ln_b[i]), w_proj_conv[i].astype(BF16), row(b_proj_conv[i]),
                        w_proj_mem[i].astype(BF16), w_out[i].astype(BF16),
                        row(g_final) if i == depth - 1 else None)
    return x
```

```python
import functools
import math

import numpy as np
import jax
import jax.numpy as jnp
from jax import lax
from jax.experimental import pallas as pl
from jax.experimental.pallas import tpu as pltpu

F32 = jnp.float32
BF16 = jnp.bfloat16

D_MODEL = 1024
N_MEM = 256
HEAD = 64
PAIR = 2 * HEAD
N_PAIR = D_MODEL // PAIR
CHUNK = 64
CHUNK_GROUP = 2
LORA_IN = 64
VRES_IN = 32
CONV_K = 31
CONV_HALO = 32
MEM_HEADS = 4
MEM_HEAD_DIM = D_MODEL // MEM_HEADS
RMS_EPS = 1e-6
LN_EPS = 1e-5
GN_EPS = 64e-5
SEG = 256
TOKEN_BLOCK = 256
VMEM_LIMIT = 56 * 1024 * 1024
INVERSE_LEVELS = (2, 4, 8, 16, 32)
M_STRICT, M_INCL, M_SAME, M_EYE, M_BLK2, M_OFF0 = 0, 1, 2, 3, 4, 5


def _mm(a, b):
    return jnp.dot(a.astype(BF16), b.astype(BF16), preferred_element_type=F32)


def _bmm(a, b):
    return lax.dot_general(a.astype(BF16), b.astype(BF16), (((2,), (1,)), ((0,), (0,))),
                           preferred_element_type=F32)


def _bmm_nt(a, b):
    return lax.dot_general(a.astype(BF16), b.astype(BF16), (((2,), (2,)), ((0,), (0,))),
                           preferred_element_type=F32)


def _bmm_tn(a, b):
    return lax.dot_general(a.astype(BF16), b.astype(BF16), (((1,), (1,)), ((0,), (0,))),
                           preferred_element_type=F32)


def _rms_norm(x, g):
    return x * lax.rsqrt(jnp.mean(x * x, axis=-1, keepdims=True) + RMS_EPS) * g


def _sigmoid(x):
    return 1.0 / (1.0 + jnp.exp(-x))


def _silu(x):
    return x * _sigmoid(x)


def _split3(x):
    hi = x.astype(BF16)
    r1 = x - hi.astype(F32)
    mid = r1.astype(BF16)
    lo = (r1 - mid.astype(F32)).astype(BF16)
    return hi, mid, lo


def _seg_sum(x, ones_bd):
    hi = x.astype(BF16)
    lo = (x - hi.astype(F32)).astype(BF16)
    outs = []
    for g in range(D_MODEL // SEG):
        sl = slice(g * SEG, (g + 1) * SEG)
        outs.append(jnp.dot(hi[:, sl], ones_bd, preferred_element_type=F32)
                    + jnp.dot(lo[:, sl], ones_bd, preferred_element_type=F32))
    return jnp.concatenate(outs, axis=1)


def _rwkv_kernel(has_vres, tb, *refs):
    if has_vres:
        (x_ref, vf_ref, g_ref, wa_ref, mu_ref, wl_ref, w0_ref, a0_ref, v0_ref, kk_ref, ka_ref, rk_ref,
         gng_ref, gnb_ref, wp_ref, cm_ref, tri_ref, ones_ref,
         y_ref, *scratch) = refs
        vout_ref = None
    else:
        (x_ref, g_ref, wa_ref, mu_ref, wl_ref, w0_ref, a0_ref, kk_ref, ka_ref, rk_ref,
         gng_ref, gnb_ref, wp_ref, cm_ref, tri_ref, ones_ref,
         y_ref, vout_ref, *scratch) = refs
        vf_ref = v0_ref = None
    (carry_ref, state_ref, at_ref, rt_ref, bt_ref, kt_ref, bh_ref, kh_ref, vv_ref, pc_ref, wkv_ref,
     lhs_ref, bk_ref, vbd_ref, tinv_ref, zl_ref, arbk_ref) = scratch
    lw_in = wl_ref.shape[0]
    sh = 3 * D_MODEL + lw_in

    @pl.when(pl.program_id(1) == 0)
    def _():
        carry_ref[...] = jnp.zeros_like(carry_ref)
        state_ref[...] = jnp.zeros_like(state_ref)

    x = x_ref[...]
    h = _rms_norm(x, g_ref[...]).astype(BF16)
    ps = jnp.dot(h, wa_ref[:, :sh], preferred_element_type=F32)
    gate = jnp.dot(h, wa_ref[:, sh:], preferred_element_type=F32)

    row = lax.broadcasted_iota(jnp.int32, (tb, 1), 0)
    prev = jnp.where(row == 0, carry_ref[0:1, :], pltpu.roll(ps, 1, 0))
    carry_ref[0:1, :] = ps[tb - 1:tb, :]
    shifted = ps + (prev - ps) * mu_ref[...]
    r = shifted[:, 0:D_MODEL]
    k = shifted[:, D_MODEL:2 * D_MODEL]
    v = shifted[:, 2 * D_MODEL:3 * D_MODEL]
    lo_in = shifted[:, 3 * D_MODEL:sh]
    lane = lax.broadcasted_iota(jnp.int32, lo_in.shape, 1)
    lo_act = jnp.where(lane < LORA_IN, jnp.tanh(lo_in), lo_in)
    lora = _mm(lo_act, wl_ref[...])

    logw = -math.exp(-0.5) * _sigmoid(w0_ref[...] + lora[:, 0:D_MODEL])
    a_icl = _sigmoid(a0_ref[...] + lora[:, D_MODEL:2 * D_MODEL])
    if has_vres:
        v = v + (vf_ref[...] - v) * _sigmoid(v0_ref[...] + lora[:, 2 * D_MODEL:3 * D_MODEL])
    else:
        vout_ref[...] = v
    ones_bd = ones_ref[...]
    kk = k * kk_ref[...]
    kk = kk / jnp.maximum(jnp.sqrt(_seg_sum(kk * kk, ones_bd)), 1e-12)
    k2 = k * (1.0 + (a_icl - 1.0) * ka_ref[...])
    b_vec = kk * a_icl

    tri = tri_ref[...]
    hi, mid, lo = _split3(logw)
    cum_tot = (jnp.dot(tri, hi, preferred_element_type=F32) + jnp.dot(tri, mid, preferred_element_type=F32)
               + jnp.dot(tri, lo, preferred_element_type=F32))
    cum = cum_tot[0:tb]
    tot = cum_tot[tb:2 * tb]
    e_k = jnp.exp(-cum)
    e_h = jnp.exp(tot - cum)

    def to_pairs(ref, val):
        for p in range(N_PAIR):
            ref[p] = val[:, p * PAIR:(p + 1) * PAIR]

    to_pairs(at_ref, (-kk * jnp.exp(cum - logw)).astype(BF16))
    to_pairs(rt_ref, (r * jnp.exp(cum)).astype(BF16))
    to_pairs(bt_ref, (b_vec * e_k).astype(BF16))
    to_pairs(kt_ref, (k2 * e_k).astype(BF16))
    to_pairs(bh_ref, (b_vec * e_h).astype(BF16))
    to_pairs(kh_ref, (k2 * e_h).astype(BF16))
    to_pairs(vv_ref, v.astype(BF16))
    to_pairs(pc_ref, jnp.exp(tot))

    is_even = lax.broadcasted_iota(jnp.int32, (1, CHUNK, PAIR), 2) < HEAD

    def dup_mask(t):
        zero = jnp.zeros_like(t)
        return jnp.concatenate([jnp.where(is_even, t, zero), jnp.where(is_even, zero, t)], axis=1)

    m_strict = cm_ref[M_STRICT]
    m_incl = cm_ref[M_INCL]
    m_same = cm_ref[M_SAME]
    n_grp = CHUNK_GROUP * N_PAIR

    def group_body(g, carry):
        def load(ref):
            parts = [ref[:, pl.ds(pl.multiple_of((g * CHUNK_GROUP + j) * CHUNK, CHUNK), CHUNK), :]
                     for j in range(CHUNK_GROUP)]
            return jnp.concatenate(parts, axis=0)

        out = pl.ds(pl.multiple_of(g * n_grp, n_grp), n_grp)
        lhs = jnp.concatenate([dup_mask(load(at_ref)), dup_mask(load(rt_ref))], axis=1)
        bt = load(bt_ref)
        kt = load(kt_ref)
        gmat = _bmm_nt(lhs, jnp.concatenate([bt, bt, kt, kt], axis=1))
        xab = gmat[:, 0:PAIR, 0:PAIR] * m_strict
        aak = gmat[:, 0:PAIR, PAIR:2 * PAIR] * m_strict
        arbk_ref[out] = jnp.concatenate([gmat[:, PAIR:2 * PAIR, 0:PAIR] * m_incl,
                                         gmat[:, PAIR:2 * PAIR, PAIR:2 * PAIR] * m_incl], axis=2).astype(BF16)
        tinv = cm_ref[M_EYE] + xab * cm_ref[M_BLK2]
        for lvl in range(len(INVERSE_LEVELS)):
            tb16 = tinv.astype(BF16)
            tinv = tinv + _bmm(_bmm(tb16, xab * cm_ref[M_OFF0 + lvl]), tb16)
        vbd = dup_mask(load(vv_ref))
        lhs_ref[out] = lhs
        vbd_ref[out] = vbd
        tinv_ref[out] = tinv.astype(BF16)
        zl_ref[out] = _bmm(aak, vbd)
        bk_ref[out] = jnp.concatenate([dup_mask(load(bh_ref)), dup_mask(load(kh_ref))], axis=1)
        return carry

    lax.fori_loop(0, tb // (CHUNK * CHUNK_GROUP), group_body, 0)

    def chunk_body(c, carry):
        sel = pl.ds(pl.multiple_of(c * N_PAIR, N_PAIR), N_PAIR)
        rows = pl.ds(pl.multiple_of(c * CHUNK, CHUNK), CHUNK)
        s0 = state_ref[...]
        ar = _bmm_nt(lhs_ref[sel], s0)
        u = _bmm(tinv_ref[sel], ar[:, 0:PAIR] + zl_ref[sel]) * m_same
        uv = jnp.concatenate([u.astype(BF16), vbd_ref[sel]], axis=1)
        y = ar[:, PAIR:2 * PAIR] + _bmm(arbk_ref[sel], uv)
        wkv_ref[:, rows, :] = jnp.where(is_even, y[:, 0:CHUNK], y[:, CHUNK:PAIR])
        pc = pc_ref[:, pl.ds(pl.multiple_of(c * CHUNK, CHUNK), 1), :]
        state_ref[...] = s0 * pc + _bmm_tn(uv, bk_ref[sel])
        return carry

    lax.fori_loop(0, tb // CHUNK, chunk_body, 0)

    wkv = jnp.concatenate([wkv_ref[p] for p in range(N_PAIR)], axis=1)
    inv_n = 1.0 / HEAD
    mean = _seg_sum(wkv, ones_bd) * inv_n
    dev = wkv - mean
    var = _seg_sum(dev * dev, ones_bd) * inv_n
    wkv_n = dev * lax.rsqrt(var + GN_EPS) * gng_ref[...] + gnb_ref[...]
    bonus = _seg_sum(r * k2 * rk_ref[...], ones_bd) * v
    y_ref[...] = _mm((wkv_n + bonus) * _silu(gate), wp_ref[...])


def _const_spec(shape):
    nd = len(shape)
    return pl.BlockSpec(shape, lambda b, s: (0,) * nd, pipeline_mode=pl.Buffered(1))


def _rwkv_call(x, v_first, g, wa, mu, wl, w0, a0, v0, k_k, k_a, r_k, gn_g, gn_b, wp, cm, tri, ones_bd):
    bsz, seq, d = x.shape
    tb = min(TOKEN_BLOCK, seq)
    has_vres = v_first is not None
    blk = pl.BlockSpec((None, tb, d), lambda b, s: (b, s, 0))
    sh = mu.shape[1]
    consts = [g, wa, mu, wl, w0, a0] + ([v0] if has_vres else []) + [k_k, k_a, r_k, gn_g, gn_b, wp, cm, tri, ones_bd]
    in_specs = [blk] + ([blk] if has_vres else []) + [_const_spec(c.shape) for c in consts]
    args = [x] + ([v_first] if has_vres else []) + consts
    out_sds = jax.ShapeDtypeStruct((bsz, seq, d), F32)
    big = lambda dt: pltpu.VMEM((N_PAIR, tb, PAIR), dt)
    n_mat = (tb // CHUNK) * N_PAIR
    mat = lambda r, c, dt: pltpu.VMEM((n_mat, r, c), dt)
    scratch = [pltpu.VMEM((8, sh), F32), pltpu.VMEM((N_PAIR, PAIR, PAIR), F32),
               big(BF16), big(BF16), big(BF16), big(BF16), big(BF16), big(BF16), big(BF16), big(F32), big(F32),
               mat(2 * PAIR, PAIR, BF16), mat(2 * PAIR, PAIR, BF16), mat(PAIR, PAIR, BF16), mat(PAIR, PAIR, BF16),
               mat(PAIR, PAIR, F32), mat(PAIR, 2 * PAIR, BF16)]
    return pl.pallas_call(
        functools.partial(_rwkv_kernel, has_vres, tb),
        grid=(bsz, seq // tb),
        in_specs=in_specs,
        out_specs=blk if has_vres else (blk, blk),
        out_shape=out_sds if has_vres else (out_sds, out_sds),
        scratch_shapes=scratch,
        compiler_params=pltpu.CompilerParams(dimension_semantics=("arbitrary", "arbitrary"),
                                             vmem_limit_bytes=VMEM_LIMIT),
        name="rwkv_vres" if has_vres else "rwkv",
    )(*args)


def _token_kernel(final_norm, tb, *refs):
    if final_norm:
        (x_ref, yr_ref, kv_ref, g_ref, wb_ref, bglu_ref, wdw_ref, bdw_ref, lng_ref, lnb_ref, wpc_ref, bpc_ref,
         wpm_ref, wout_ref, gfin_ref, out_ref, ubuf_ref) = refs
    else:
        (x_ref, yr_ref, kv_ref, g_ref, wb_ref, bglu_ref, wdw_ref, bdw_ref, lng_ref, lnb_ref, wpc_ref, bpc_ref,
         wpm_ref, wout_ref, out_ref, ubuf_ref) = refs
        gfin_ref = None
    d = D_MODEL

    @pl.when(pl.program_id(1) == 0)
    def _():
        ubuf_ref[0:CONV_HALO, :] = jnp.zeros((CONV_HALO, d), F32)

    x = x_ref[...]
    h = _rms_norm(x, g_ref[...]).astype(BF16)

    glu = jnp.dot(h, wb_ref[:, 0:2 * d], preferred_element_type=F32) + bglu_ref[...]
    ubuf_ref[CONV_HALO:CONV_HALO + tb, :] = glu[:, 0:d] * _sigmoid(glu[:, d:2 * d])
    acc = jnp.zeros((tb, d), F32) + bdw_ref[...]
    for j in range(CONV_K):
        off = CONV_HALO - (CONV_K - 1) + j
        acc = acc + ubuf_ref[off:off + tb, :] * wdw_ref[j:j + 1, :]
    ubuf_ref[0:CONV_HALO, :] = ubuf_ref[tb:tb + CONV_HALO, :]
    um = jnp.mean(acc, axis=-1, keepdims=True)
    ud = acc - um
    uv = jnp.mean(ud * ud, axis=-1, keepdims=True)
    u = ud * lax.rsqrt(uv + LN_EPS) * lng_ref[...] + lnb_ref[...]
    cgate = jnp.dot(h, wb_ref[:, 2 * d:3 * d], preferred_element_type=F32)
    y_conv = _mm(_silu(u) * _silu(cgate), wpc_ref[...]) + bpc_ref[...]

    q = jnp.dot(h, wb_ref[:, 3 * d:4 * d], preferred_element_type=F32).astype(BF16)
    heads = []
    for hd in range(MEM_HEADS):
        sl = slice(hd * MEM_HEAD_DIM, (hd + 1) * MEM_HEAD_DIM)
        sc = lax.dot_general(q[:, sl], kv_ref[:, sl], (((1,), (1,)), ((), ())),
                             preferred_element_type=F32) * (MEM_HEAD_DIM ** -0.5)
        sc = jnp.exp(sc - jnp.max(sc, axis=-1, keepdims=True))
        prob = sc / jnp.sum(sc, axis=-1, keepdims=True)
        heads.append(jnp.dot(prob.astype(BF16), kv_ref[:, d + hd * MEM_HEAD_DIM:d + (hd + 1) * MEM_HEAD_DIM],
                             preferred_element_type=F32))
    att = jnp.concatenate(heads, axis=1)
    mgate = jnp.dot(h, wb_ref[:, 4 * d:5 * d], preferred_element_type=F32)
    y_mem = _mm(att * _silu(mgate), wpm_ref[...])

    y = _sigmoid(jnp.dot(h, wb_ref[:, 5 * d:6 * d], preferred_element_type=F32)) * yr_ref[...]
    y = y + _sigmoid(jnp.dot(h, wb_ref[:, 6 * d:7 * d], preferred_element_type=F32)) * y_conv
    y = y + _sigmoid(jnp.dot(h, wb_ref[:, 7 * d:8 * d], preferred_element_type=F32)) * y_mem
    out = x + _mm(y, wout_ref[...])
    if final_norm:
        out = _rms_norm(out, gfin_ref[...])
    out_ref[...] = out


def _token_call(x, y_rwkv, kv, g, wb, b_glu, w_dw, b_dw, ln_g, ln_b, wpc, bpc, wpm, wout, g_final):
    bsz, seq, d = x.shape
    tb = min(TOKEN_BLOCK, seq)
    final_norm = g_final is not None
    blk = pl.BlockSpec((None, tb, d), lambda b, s: (b, s, 0))
    kv_spec = pl.BlockSpec((None, N_MEM, 2 * d), lambda b, s: (b, 0, 0))
    consts = [g, wb, b_glu, w_dw, b_dw, ln_g, ln_b, wpc, bpc, wpm, wout] + ([g_final] if final_norm else [])
    return pl.pallas_call(
        functools.partial(_token_kernel, final_norm, tb),
        grid=(bsz, seq // tb),
        in_specs=[blk, blk, kv_spec] + [_const_spec(c.shape) for c in consts],
        out_specs=blk,
        out_shape=jax.ShapeDtypeStruct((bsz, seq, d), F32),
        scratch_shapes=[pltpu.VMEM((tb + CONV_HALO, d), F32)],
        compiler_params=pltpu.CompilerParams(dimension_semantics=("arbitrary", "arbitrary"),
                                             vmem_limit_bytes=VMEM_LIMIT),
        name="token_final" if final_norm else "token",
    )(x, y_rwkv, kv, *consts)


def _memkv_kernel(mem_ref, g_ref, w_ref, kv_ref):
    m = _rms_norm(mem_ref[...], g_ref[...])
    kv_ref[...] = _mm(m, w_ref[...]).astype(BF16)


def _memkv_call(mem, g_mem, w_kv):
    bsz, n_mem, d = mem.shape
    depth = g_mem.shape[0]
    return pl.pallas_call(
        _memkv_kernel,
        grid=(depth, bsz),
        in_specs=[pl.BlockSpec((None, n_mem, d), lambda l, b: (b, 0, 0)),
                  pl.BlockSpec((None, 1, d), lambda l, b: (l, 0, 0)),
                  pl.BlockSpec((None, d, 2 * d), lambda l, b: (l, 0, 0))],
        out_specs=pl.BlockSpec((None, None, n_mem, 2 * d), lambda l, b: (l, b, 0, 0)),
        out_shape=jax.ShapeDtypeStruct((depth, bsz, n_mem, 2 * d), BF16),
        compiler_params=pltpu.CompilerParams(dimension_semantics=("arbitrary", "arbitrary"),
                                             vmem_limit_bytes=VMEM_LIMIT),
        name="mem_kv",
    )(mem, g_mem, w_kv)


def _pair_masks():
    i = np.arange(PAIR)
    row, col = i[:, None], i[None, :]
    same = (row // HEAD) == (col // HEAD)
    rl, cl = row % HEAD, col % HEAD
    masks = [same & (cl < rl), same & (cl <= rl), same, row == col, same & (rl // 2 == cl // 2) & (cl < rl)]
    for n in INVERSE_LEVELS:
        masks.append(same & (rl // (2 * n) == cl // (2 * n)) & ((rl // n) % 2 == 1) & ((cl // n) % 2 == 0))
    return jnp.asarray(np.stack(masks).astype(np.float32))


def _cumsum_matrix(tb):
    i = np.arange(tb)
    same = (i[:, None] // CHUNK) == (i[None, :] // CHUNK)
    tri = same & (i[None, :] <= i[:, None])
    return jnp.asarray(np.concatenate([tri, same], axis=0).astype(np.float32)).astype(BF16)


def _segment_ones():
    i = np.arange(SEG)
    return jnp.asarray(((i[:, None] // HEAD) == (i[None, :] // HEAD)).astype(np.float32)).astype(BF16)


def kernel(x, mem, g_norm, w_in, mu_shift, w0, w_decay_up, a0, w_aaa_up, k_k, k_a, r_k, gn_g, gn_b, w_proj_rwkv, w_vres_down, mu_vres, v0, w_vres_up, b_glu, w_dw, b_dw, ln_g, ln_b, w_proj_conv, b_proj_conv, g_mem_norm, w_mem_kv, w_proj_mem, w_out, g_final):
    depth, d = g_norm.shape
    seq = x.shape[1]
    tb = min(TOKEN_BLOCK, seq)
    n_shift = 3 * d + 2 * LORA_IN
    row = lambda a: a.reshape(1, -1).astype(F32)
    cm, tri, ones_bd = _pair_masks(), _cumsum_matrix(tb), _segment_ones()
    kv = _memkv_call(mem, g_mem_norm.reshape(depth, 1, d), w_mem_kv.astype(BF16))

    v_first = None
    for i in range(depth):
        has_vres = i > 0
        lw_in = 2 * PAIR if has_vres else PAIR
        w_shift, w_rest = w_in[i][:, :n_shift], w_in[i][:, n_shift:]
        wl = jnp.zeros((lw_in, (3 if has_vres else 2) * d), F32)
        wl = wl.at[0:LORA_IN, 0:d].set(w_decay_up[i]).at[LORA_IN:2 * LORA_IN, d:2 * d].set(w_aaa_up[i])
        if has_vres:
            pad_w = jnp.zeros((d, PAIR - VRES_IN), F32)
            wa = jnp.concatenate([w_shift, w_vres_down[i - 1], pad_w, w_rest[:, :d]], axis=1)
            mu = jnp.concatenate([mu_shift[i], mu_vres[i - 1], jnp.zeros((PAIR - VRES_IN,), F32)])
            wl = wl.at[PAIR:PAIR + VRES_IN, 2 * d:3 * d].set(w_vres_up[i - 1])
        else:
            wa = jnp.concatenate([w_shift, w_rest[:, :d]], axis=1)
            mu = mu_shift[i]
        outs = _rwkv_call(x, v_first, row(g_norm[i]), wa.astype(BF16), row(mu), wl.astype(BF16), row(w0[i]),
                          row(a0[i]), row(v0[i - 1]) if has_vres else None, row(k_k[i]), row(k_a[i]), row(r_k[i]),
                          row(gn_g[i]), row(gn_b[i]), w_proj_rwkv[i].astype(BF16), cm, tri, ones_bd)
        if has_vres:
            y_rwkv = outs
        else:
            y_rwkv, v_first = outs
        w_dw_pad = jnp.concatenate([w_dw[i], jnp.zeros((CONV_HALO - CONV_K, d), F32)], axis=0)
        x = _token_call(x, y_rwkv, kv[i], row(g_norm[i]), w_rest[:, d:].astype(BF16), row(b_glu[i]), w_dw_pad,
                        row(b_dw[i]), row(ln_g[i]), row(ln_b[i]), w_proj_conv[i].astype(BF16), row(b_proj_conv[i]),
                        w_proj_mem[i].astype(BF16), w_out[i].astype(BF16),
                        row(g_final) if i == depth - 1 else None)
    return x
```

```python
import functools
import math

import numpy as np
import jax
import jax.numpy as jnp
from jax import lax
from jax.experimental import pallas as pl
from jax.experimental.pallas import tpu as pltpu

F32 = jnp.float32
BF16 = jnp.bfloat16

D_MODEL = 1024
N_MEM = 256
HEAD = 64
PAIR = 2 * HEAD
N_PAIR = D_MODEL // PAIR
CHUNK = 64
CHUNK_GROUP = 4
LORA_IN = 64
VRES_IN = 32
CONV_K = 31
CONV_HALO = 32
CONV_ROWS = 64
SUBLANE = 8
LANE = 128
MEM_HEADS = 4
MEM_HEAD_DIM = D_MODEL // MEM_HEADS
RMS_EPS = 1e-6
LN_EPS = 1e-5
GN_EPS = 64e-5
SEG = 256
TOKEN_BLOCK = 256
VMEM_LIMIT = 56 * 1024 * 1024
INVERSE_LEVELS = (2, 4, 8, 16, 32)
MS_STRICT, MS_INCL, MS_EYE, MS_BLK2, M_OFF0 = 0, 1, 2, 3, 4


def _mm(a, b):
    return jnp.dot(a.astype(BF16), b.astype(BF16), preferred_element_type=F32)


def _bmm(a, b):
    return lax.dot_general(a.astype(BF16), b.astype(BF16), (((2,), (1,)), ((0,), (0,))),
                           preferred_element_type=F32)


def _bmm_nt(a, b):
    return lax.dot_general(a.astype(BF16), b.astype(BF16), (((2,), (2,)), ((0,), (0,))),
                           preferred_element_type=F32)


def _bmm_tn(a, b):
    return lax.dot_general(a.astype(BF16), b.astype(BF16), (((1,), (1,)), ((0,), (0,))),
                           preferred_element_type=F32)


def _rms_norm(x, g):
    return x * lax.rsqrt(jnp.mean(x * x, axis=-1, keepdims=True) + RMS_EPS) * g


def _sigmoid(x):
    return 1.0 / (1.0 + jnp.exp(-x))


def _silu(x):
    return x * _sigmoid(x)


def _split2(x):
    hi = x.astype(BF16)
    return hi, (x - hi.astype(F32)).astype(BF16)


def _seg_sum(x, ones_bd):
    xb = x.astype(BF16)
    outs = [jnp.dot(xb[:, g * SEG:(g + 1) * SEG], ones_bd, preferred_element_type=F32)
            for g in range(D_MODEL // SEG)]
    return jnp.concatenate(outs, axis=1)


def _rwkv_kernel(has_vres, tb, *refs):
    if has_vres:
        (x_ref, vf_ref, g_ref, wa_ref, mu_ref, wl_ref, w0_ref, a0_ref, v0_ref, kk_ref, ka_ref, rk_ref,
         gng_ref, gnb_ref, wp_ref, cm_ref, cm16_ref, tri_ref, ones_ref,
         y_ref, *scratch) = refs
        vout_ref = None
    else:
        (x_ref, g_ref, wa_ref, mu_ref, wl_ref, w0_ref, a0_ref, kk_ref, ka_ref, rk_ref,
         gng_ref, gnb_ref, wp_ref, cm_ref, cm16_ref, tri_ref, ones_ref,
         y_ref, vout_ref, *scratch) = refs
        vf_ref = v0_ref = None
    (carry_ref, state_ref, at_ref, rt_ref, bt_ref, kt_ref, bh_ref, kh_ref, vv_ref, pc_ref, wkv_ref,
     rp_ref, yl_ref, mm_ref, gm_ref) = scratch
    lw_in = wl_ref.shape[0]
    sh = 3 * D_MODEL + lw_in

    @pl.when(pl.program_id(1) == 0)
    def _():
        carry_ref[...] = jnp.zeros_like(carry_ref)
        state_ref[...] = jnp.zeros_like(state_ref)

    x = x_ref[...]
    h = _rms_norm(x, g_ref[...]).astype(BF16)
    ps = jnp.dot(h, wa_ref[:, :sh], preferred_element_type=F32)
    gate = jnp.dot(h, wa_ref[:, sh:], preferred_element_type=F32)

    row = lax.broadcasted_iota(jnp.int32, (tb, 1), 0)
    prev = jnp.where(row == 0, carry_ref[0:1, :], pltpu.roll(ps, 1, 0))
    carry_ref[0:1, :] = ps[tb - 1:tb, :]
    shifted = ps + (prev - ps) * mu_ref[...]
    r = shifted[:, 0:D_MODEL]
    k = shifted[:, D_MODEL:2 * D_MODEL]
    v = shifted[:, 2 * D_MODEL:3 * D_MODEL]
    lo_in = shifted[:, 3 * D_MODEL:sh]
    lane = lax.broadcasted_iota(jnp.int32, lo_in.shape, 1)
    lo_act = jnp.where(lane < LORA_IN, jnp.tanh(lo_in), lo_in)
    lora = _mm(lo_act, wl_ref[...])

    logw = -math.exp(-0.5) * _sigmoid(w0_ref[...] + lora[:, 0:D_MODEL])
    a_icl = _sigmoid(a0_ref[...] + lora[:, D_MODEL:2 * D_MODEL])
    if has_vres:
        v = v + (vf_ref[...] - v) * _sigmoid(v0_ref[...] + lora[:, 2 * D_MODEL:3 * D_MODEL])
    else:
        vout_ref[...] = v
    ones_bd = ones_ref[...]
    kk = k * kk_ref[...]
    kk = kk / jnp.maximum(jnp.sqrt(_seg_sum(kk * kk, ones_bd)), 1e-12)
    k2 = k * (1.0 + (a_icl - 1.0) * ka_ref[...])
    b_vec = kk * a_icl

    tri = tri_ref[...]
    hi, lo = _split2(logw)
    cum = jnp.dot(tri, hi, preferred_element_type=F32) + jnp.dot(tri, lo, preferred_element_type=F32)
    n_chunk = tb // CHUNK
    tot_rows = [cum[(c + 1) * CHUNK - 1:(c + 1) * CHUNK, :] for c in range(n_chunk)]
    tot = jnp.concatenate([jnp.broadcast_to(t, (CHUNK, D_MODEL)) for t in tot_rows], axis=0)
    e_k = jnp.exp(-cum)
    e_h = jnp.exp(tot - cum)

    def to_pairs(ref, val):
        for p in range(N_PAIR):
            ref[p] = val[:, p * PAIR:(p + 1) * PAIR]

    to_pairs(at_ref, (-kk * jnp.exp(cum - logw)).astype(BF16))
    to_pairs(rt_ref, (r * jnp.exp(cum)).astype(BF16))
    to_pairs(bt_ref, (b_vec * e_k).astype(BF16))
    to_pairs(kt_ref, (k2 * e_k).astype(BF16))
    to_pairs(bh_ref, (b_vec * e_h).astype(BF16))
    to_pairs(kh_ref, (k2 * e_h).astype(BF16))
    to_pairs(vv_ref, v.astype(BF16))
    for c in range(n_chunk):
        p_c = jnp.exp(tot_rows[c])
        for p in range(N_PAIR):
            pc_ref[c, p] = jnp.broadcast_to(p_c[:, p * PAIR:(p + 1) * PAIR], (8, PAIR))

    def block_diag(t):
        t2 = jnp.concatenate([t, t], axis=1)
        same_head = (lax.broadcasted_iota(jnp.int32, (1,) + t2.shape[1:], 1) // HEAD
                     == (lax.broadcasted_iota(jnp.int32, (1,) + t2.shape[1:], 2) % PAIR) // HEAD)
        return jnp.where(same_head, t2, jnp.zeros_like(t2))

    ms_strict = cm_ref[MS_STRICT, 0:CHUNK, :]
    ms_incl = cm_ref[MS_INCL, 0:CHUNK, :]
    n_grp = CHUNK_GROUP * N_PAIR

    def group_body(g, carry):
        def load(ref):
            parts = [ref[:, pl.ds(pl.multiple_of((g * CHUNK_GROUP + j) * CHUNK, CHUNK), CHUNK), :]
                     for j in range(CHUNK_GROUP)]
            return jnp.concatenate(parts, axis=0)

        out = pl.ds(pl.multiple_of(g * n_grp, n_grp), n_grp)
        at, rt = load(at_ref), load(rt_ref)
        gmat = _bmm_nt(jnp.concatenate([at, rt], axis=1),
                       jnp.concatenate([block_diag(load(bt_ref)), block_diag(load(kt_ref))], axis=1))
        xab = gmat[:, 0:CHUNK, 0:PAIR] * ms_strict
        aak = gmat[:, 0:CHUNK, PAIR:2 * PAIR] * ms_strict
        arbk = jnp.concatenate([gmat[:, CHUNK:PAIR, 0:PAIR] * ms_incl,
                                gmat[:, CHUNK:PAIR, PAIR:2 * PAIR] * ms_incl], axis=2).astype(BF16)
        xab16 = xab.astype(BF16)
        xab2 = jnp.concatenate([xab16, xab16], axis=1)
        tinv = cm_ref[MS_EYE, 0:CHUNK, :] + xab * cm_ref[MS_BLK2, 0:CHUNK, :]
        for lvl in range(len(INVERSE_LEVELS)):
            t16 = tinv.astype(BF16)
            tinv = tinv + _bmm(_bmm(t16, xab2 * cm16_ref[M_OFF0 + lvl]), block_diag(t16))
        vbd = block_diag(load(vv_ref))
        zl = _bmm(aak, vbd).astype(BF16)
        aw = _bmm(tinv, jnp.concatenate([block_diag(at), block_diag(zl)], axis=2))
        q = jnp.concatenate([block_diag(aw.astype(BF16)),
                             jnp.concatenate([jnp.zeros_like(vbd), vbd], axis=2)], axis=1)
        ry = _bmm(arbk, q)
        rp_ref[out] = (rt.astype(F32) + ry[:, :, 0:PAIR]).astype(BF16)
        yl_ref[out] = ry[:, :, PAIR:2 * PAIR]
        mg = _bmm_tn(q, jnp.concatenate([block_diag(load(bh_ref)), block_diag(load(kh_ref))], axis=1))
        mm_ref[out] = mg[:, 0:PAIR].astype(BF16)
        gm_ref[out] = mg[:, PAIR:2 * PAIR]
        return carry

    lax.fori_loop(0, tb // (CHUNK * CHUNK_GROUP), group_body, 0)

    def chunk_body(c, carry):
        sel = pl.ds(pl.multiple_of(c * N_PAIR, N_PAIR), N_PAIR)
        s0 = state_ref[...]
        s16 = s0.astype(BF16)
        wkv_ref[:, pl.ds(pl.multiple_of(c * CHUNK, CHUNK), CHUNK), :] = _bmm_nt(rp_ref[sel], s16) + yl_ref[sel]
        state_ref[...] = s0 * pc_ref[c][:, 0:1, :] + _bmm(s16, mm_ref[sel]) + gm_ref[sel]
        return carry

    lax.fori_loop(0, n_chunk, chunk_body, 0)

    wkv = jnp.concatenate([wkv_ref[p] for p in range(N_PAIR)], axis=1)
    inv_n = 1.0 / HEAD
    mean = _seg_sum(wkv, ones_bd) * inv_n
    dev = wkv - mean
    var = _seg_sum(dev * dev, ones_bd) * inv_n
    wkv_n = dev * lax.rsqrt(var + GN_EPS) * gng_ref[...] + gnb_ref[...]
    bonus = _seg_sum(r * k2 * rk_ref[...], ones_bd) * v
    y_ref[...] = _mm((wkv_n + bonus) * _silu(gate), wp_ref[...])


def _const_spec(shape):
    nd = len(shape)
    return pl.BlockSpec(shape, lambda b, s: (0,) * nd, pipeline_mode=pl.Buffered(1))


def _rwkv_call(x, v_first, g, wa, mu, wl, w0, a0, v0, k_k, k_a, r_k, gn_g, gn_b, wp, cm, tri, ones_bd):
    bsz, seq, d = x.shape
    tb = min(TOKEN_BLOCK, seq)
    has_vres = v_first is not None
    blk = pl.BlockSpec((None, tb, d), lambda b, s: (b, s, 0))
    sh = mu.shape[1]
    consts = ([g, wa, mu, wl, w0, a0] + ([v0] if has_vres else [])
              + [k_k, k_a, r_k, gn_g, gn_b, wp, cm, cm.astype(BF16), tri, ones_bd])
    in_specs = [blk] + ([blk] if has_vres else []) + [_const_spec(c.shape) for c in consts]
    args = [x] + ([v_first] if has_vres else []) + consts
    out_sds = jax.ShapeDtypeStruct((bsz, seq, d), F32)
    big = lambda dt: pltpu.VMEM((N_PAIR, tb, PAIR), dt)
    n_chunk = tb // CHUNK
    mat = lambda r, c, dt: pltpu.VMEM((n_chunk * N_PAIR, r, c), dt)
    scratch = [pltpu.VMEM((8, sh), F32), pltpu.VMEM((N_PAIR, PAIR, PAIR), F32),
               big(BF16), big(BF16), big(BF16), big(BF16), big(BF16), big(BF16), big(BF16),
               pltpu.VMEM((n_chunk, N_PAIR, 8, PAIR), F32), big(F32),
               mat(CHUNK, PAIR, BF16), mat(CHUNK, PAIR, F32), mat(PAIR, PAIR, BF16), mat(PAIR, PAIR, F32)]
    return pl.pallas_call(
        functools.partial(_rwkv_kernel, has_vres, tb),
        grid=(bsz, seq // tb),
        in_specs=in_specs,
        out_specs=blk if has_vres else (blk, blk),
        out_shape=out_sds if has_vres else (out_sds, out_sds),
        scratch_shapes=scratch,
        compiler_params=pltpu.CompilerParams(dimension_semantics=("arbitrary", "arbitrary"),
                                             vmem_limit_bytes=VMEM_LIMIT),
        name="rwkv_vres" if has_vres else "rwkv",
    )(*args)


def _token_kernel(final_norm, tb, *refs):
    if final_norm:
        (x_ref, yr_ref, kv_ref, g_ref, wb_ref, bglu_ref, wdw_ref, bdw_ref, lng_ref, lnb_ref, wpc_ref, bpc_ref,
         wpm_ref, wout_ref, gfin_ref, out_ref, ubuf_ref, ush_ref, conv_ref) = refs
    else:
        (x_ref, yr_ref, kv_ref, g_ref, wb_ref, bglu_ref, wdw_ref, bdw_ref, lng_ref, lnb_ref, wpc_ref, bpc_ref,
         wpm_ref, wout_ref, out_ref, ubuf_ref, ush_ref, conv_ref) = refs
        gfin_ref = None
    d = D_MODEL

    @pl.when(pl.program_id(1) == 0)
    def _():
        ubuf_ref[0:CONV_HALO, :] = jnp.zeros((CONV_HALO, d), F32)

    x = x_ref[...]
    h = _rms_norm(x, g_ref[...]).astype(BF16)

    glu = jnp.dot(h, wb_ref[:, 0:2 * d], preferred_element_type=F32) + bglu_ref[...]
    ubuf_ref[CONV_HALO:CONV_HALO + tb, :] = glu[:, 0:d] * _sigmoid(glu[:, d:2 * d])
    n_sh = tb + CONV_HALO - SUBLANE
    for b in range(1, SUBLANE):
        ush_ref[b - 1] = ubuf_ref[b:b + n_sh, :]

    for lt in range(d // LANE):
        lanes = slice(lt * LANE, (lt + 1) * LANE)
        for r0 in range(0, tb, CONV_ROWS):
            acc = jnp.broadcast_to(bdw_ref[:, lanes], (CONV_ROWS, LANE))
            for j in range(CONV_K):
                a, b = divmod(CONV_HALO - (CONV_K - 1) + j, SUBLANE)
                rows = slice(r0 + a * SUBLANE, r0 + a * SUBLANE + CONV_ROWS)
                tap = ubuf_ref[rows, lanes] if b == 0 else ush_ref[b - 1, rows, lanes]
                acc = acc + tap * wdw_ref[j:j + 1, lanes]
            conv_ref[r0:r0 + CONV_ROWS, lanes] = acc
    ubuf_ref[0:CONV_HALO, :] = ubuf_ref[tb:tb + CONV_HALO, :]
    acc = conv_ref[...]
    um = jnp.mean(acc, axis=-1, keepdims=True)
    ud = acc - um
    uv = jnp.mean(ud * ud, axis=-1, keepdims=True)
    u = ud * lax.rsqrt(uv + LN_EPS) * lng_ref[...] + lnb_ref[...]
    cgate = jnp.dot(h, wb_ref[:, 2 * d:3 * d], preferred_element_type=F32)
    y_conv = _mm(_silu(u) * _silu(cgate), wpc_ref[...]) + bpc_ref[...]

    q = jnp.dot(h, wb_ref[:, 3 * d:4 * d], preferred_element_type=F32).astype(BF16)
    heads = []
    for hd in range(MEM_HEADS):
        sl = slice(hd * MEM_HEAD_DIM, (hd + 1) * MEM_HEAD_DIM)
        sc = lax.dot_general(q[:, sl], kv_ref[:, sl], (((1,), (1,)), ((), ())),
                             preferred_element_type=F32) * (MEM_HEAD_DIM ** -0.5)
        sc = jnp.exp(sc - jnp.max(sc, axis=-1, keepdims=True))
        prob = sc / jnp.sum(sc, axis=-1, keepdims=True)
        heads.append(jnp.dot(prob.astype(BF16), kv_ref[:, d + hd * MEM_HEAD_DIM:d + (hd + 1) * MEM_HEAD_DIM],
                             preferred_element_type=F32))
    att = jnp.concatenate(heads, axis=1)
    mgate = jnp.dot(h, wb_ref[:, 4 * d:5 * d], preferred_element_type=F32)
    y_mem = _mm(att * _silu(mgate), wpm_ref[...])

    y = _sigmoid(jnp.dot(h, wb_ref[:, 5 * d:6 * d], preferred_element_type=F32)) * yr_ref[...]
    y = y + _sigmoid(jnp.dot(h, wb_ref[:, 6 * d:7 * d], preferred_element_type=F32)) * y_conv
    y = y + _sigmoid(jnp.dot(h, wb_ref[:, 7 * d:8 * d], preferred_element_type=F32)) * y_mem
    out = x + _mm(y, wout_ref[...])
    if final_norm:
        out = _rms_norm(out, gfin_ref[...])
    out_ref[...] = out


def _token_call(x, y_rwkv, kv, g, wb, b_glu, w_dw, b_dw, ln_g, ln_b, wpc, bpc, wpm, wout, g_final):
    bsz, seq, d = x.shape
    tb = min(TOKEN_BLOCK, seq)
    final_norm = g_final is not None
    blk = pl.BlockSpec((None, tb, d), lambda b, s: (b, s, 0))
    kv_spec = pl.BlockSpec((None, N_MEM, 2 * d), lambda b, s: (b, 0, 0))
    consts = [g, wb, b_glu, w_dw, b_dw, ln_g, ln_b, wpc, bpc, wpm, wout] + ([g_final] if final_norm else [])
    return pl.pallas_call(
        functools.partial(_token_kernel, final_norm, tb),
        grid=(bsz, seq // tb),
        in_specs=[blk, blk, kv_spec] + [_const_spec(c.shape) for c in consts],
        out_specs=blk,
        out_shape=jax.ShapeDtypeStruct((bsz, seq, d), F32),
        scratch_shapes=[pltpu.VMEM((tb + CONV_HALO, d), F32),
                        pltpu.VMEM((SUBLANE - 1, tb + CONV_HALO - SUBLANE, d), F32),
                        pltpu.VMEM((tb, d), F32)],
        compiler_params=pltpu.CompilerParams(dimension_semantics=("arbitrary", "arbitrary"),
                                             vmem_limit_bytes=VMEM_LIMIT),
        name="token_final" if final_norm else "token",
    )(x, y_rwkv, kv, *consts)


def _memkv_kernel(mem_ref, g_ref, w_ref, kv_ref):
    m = _rms_norm(mem_ref[...], g_ref[...])
    kv_ref[...] = _mm(m, w_ref[...]).astype(BF16)


def _memkv_call(mem, g_mem, w_kv):
    bsz, n_mem, d = mem.shape
    depth = g_mem.shape[0]
    return pl.pallas_call(
        _memkv_kernel,
        grid=(depth, bsz),
        in_specs=[pl.BlockSpec((None, n_mem, d), lambda l, b: (b, 0, 0)),
                  pl.BlockSpec((None, 1, d), lambda l, b: (l, 0, 0)),
                  pl.BlockSpec((None, d, 2 * d), lambda l, b: (l, 0, 0))],
        out_specs=pl.BlockSpec((None, None, n_mem, 2 * d), lambda l, b: (l, b, 0, 0)),
        out_shape=jax.ShapeDtypeStruct((depth, bsz, n_mem, 2 * d), BF16),
        compiler_params=pltpu.CompilerParams(dimension_semantics=("arbitrary", "arbitrary"),
                                             vmem_limit_bytes=VMEM_LIMIT),
        name="mem_kv",
    )(mem, g_mem, w_kv)


def _pair_masks():
    i = np.arange(PAIR)
    row, col = i[:, None], i[None, :]
    same = (row // HEAD) == (col // HEAD)
    rl, cl = row % HEAD, col % HEAD
    top = row < HEAD
    masks = [top & (cl < row), top & (cl <= row), top & (cl == row), top & (cl // 2 == row // 2) & (cl < row)]
    for n in INVERSE_LEVELS:
        masks.append(same & (rl // (2 * n) == cl // (2 * n)) & ((rl // n) % 2 == 1) & ((cl // n) % 2 == 0))
    return jnp.asarray(np.stack(masks).astype(np.float32))


def _cumsum_matrix(tb):
    i = np.arange(tb)
    tri = ((i[:, None] // CHUNK) == (i[None, :] // CHUNK)) & (i[None, :] <= i[:, None])
    return jnp.asarray(tri.astype(np.float32)).astype(BF16)


def _segment_ones():
    i = np.arange(SEG)
    return jnp.asarray(((i[:, None] // HEAD) == (i[None, :] // HEAD)).astype(np.float32)).astype(BF16)


def kernel(x, mem, g_norm, w_in, mu_shift, w0, w_decay_up, a0, w_aaa_up, k_k, k_a, r_k, gn_g, gn_b, w_proj_rwkv, w_vres_down, mu_vres, v0, w_vres_up, b_glu, w_dw, b_dw, ln_g, ln_b, w_proj_conv, b_proj_conv, g_mem_norm, w_mem_kv, w_proj_mem, w_out, g_final):
    depth, d = g_norm.shape
    seq = x.shape[1]
    tb = min(TOKEN_BLOCK, seq)
    n_shift = 3 * d + 2 * LORA_IN
    row = lambda a: a.reshape(1, -1).astype(F32)
    cm, tri, ones_bd = _pair_masks(), _cumsum_matrix(tb), _segment_ones()
    kv = _memkv_call(mem, g_mem_norm.reshape(depth, 1, d), w_mem_kv.astype(BF16))

    v_first = None
    for i in range(depth):
        has_vres = i > 0
        lw_in = 2 * PAIR if has_vres else PAIR
        w_shift, w_rest = w_in[i][:, :n_shift], w_in[i][:, n_shift:]
        wl = jnp.zeros((lw_in, (3 if has_vres else 2) * d), F32)
        wl = wl.at[0:LORA_IN, 0:d].set(w_decay_up[i]).at[LORA_IN:2 * LORA_IN, d:2 * d].set(w_aaa_up[i])
        if has_vres:
            pad_w = jnp.zeros((d, PAIR - VRES_IN), F32)
            wa = jnp.concatenate([w_shift, w_vres_down[i - 1], pad_w, w_rest[:, :d]], axis=1)
            mu = jnp.concatenate([mu_shift[i], mu_vres[i - 1], jnp.zeros((PAIR - VRES_IN,), F32)])
            wl = wl.at[PAIR:PAIR + VRES_IN, 2 * d:3 * d].set(w_vres_up[i - 1])
        else:
            wa = jnp.concatenate([w_shift, w_rest[:, :d]], axis=1)
            mu = mu_shift[i]
        outs = _rwkv_call(x, v_first, row(g_norm[i]), wa.astype(BF16), row(mu), wl.astype(BF16), row(w0[i]),
                          row(a0[i]), row(v0[i - 1]) if has_vres else None, row(k_k[i]), row(k_a[i]), row(r_k[i]),
                          row(gn_g[i]), row(gn_b[i]), w_proj_rwkv[i].astype(BF16), cm, tri, ones_bd)
        if has_vres:
            y_rwkv = outs
        else:
            y_rwkv, v_first = outs
        w_dw_pad = jnp.concatenate([w_dw[i], jnp.zeros((CONV_HALO - CONV_K, d), F32)], axis=0)
        x = _token_call(x, y_rwkv, kv[i], row(g_norm[i]), w_rest[:, d:].astype(BF16), row(b_glu[i]), w_dw_pad,
                        row(b_dw[i]), row(ln_g[i]), row(ln_b[i]), w_proj_conv[i].astype(BF16), row(b_proj_conv[i]),
                        w_proj_mem[i].astype(BF16), w_out[i].astype(BF16),
                        row(g_final) if i == depth - 1 else None)
    return x
```

```python
import functools
import math

import numpy as np
import jax
import jax.numpy as jnp
from jax import lax
from jax.experimental import pallas as pl
from jax.experimental.pallas import tpu as pltpu

F32 = jnp.float32
BF16 = jnp.bfloat16

D_MODEL = 1024
N_MEM = 256
HEAD = 64
PAIR = 2 * HEAD
N_PAIR = D_MODEL // PAIR
CHUNK = 64
CHUNK_GROUP = 4
LORA_IN = 64
VRES_IN = 32
CONV_K = 31
CONV_HALO = 32
CONV_ROWS = 64
SUBLANE = 8
LANE = 128
MXU_COLS = 256
SHIFT_COPY_COST = 5
CONV_TILE_COST = 2
MEM_HEADS = 4
MEM_HEAD_DIM = D_MODEL // MEM_HEADS
RMS_EPS = 1e-6
LN_EPS = 1e-5
GN_EPS = 64e-5
SEG = 256
TOKEN_BLOCK = 256
VMEM_LIMIT = 56 * 1024 * 1024
INVERSE_LEVELS = (2, 4, 8, 16, 32)
MS_STRICT, MS_INCL, MS_EYE, MS_BLK2, M_OFF0 = 0, 1, 2, 3, 4


def _mm(a, b):
    return jnp.dot(a.astype(BF16), b.astype(BF16), preferred_element_type=F32)


def _bmm(a, b):
    return lax.dot_general(a.astype(BF16), b.astype(BF16), (((2,), (1,)), ((0,), (0,))),
                           preferred_element_type=F32)


def _bmm_nt(a, b):
    return lax.dot_general(a.astype(BF16), b.astype(BF16), (((2,), (2,)), ((0,), (0,))),
                           preferred_element_type=F32)


def _bmm_tn(a, b):
    return lax.dot_general(a.astype(BF16), b.astype(BF16), (((1,), (1,)), ((0,), (0,))),
                           preferred_element_type=F32)


def _rms_norm(x, g):
    return x * lax.rsqrt(jnp.mean(x * x, axis=-1, keepdims=True) + RMS_EPS) * g


def _sigmoid(x):
    return 1.0 / (1.0 + jnp.exp(-x))


def _silu(x):
    return x * _sigmoid(x)


def _split2(x):
    hi = x.astype(BF16)
    return hi, (x - hi.astype(F32)).astype(BF16)


def _seg_sum(x, ones_bd):
    xb = x.astype(BF16)
    outs = [jnp.dot(xb[:, g * SEG:(g + 1) * SEG], ones_bd, preferred_element_type=F32)
            for g in range(D_MODEL // SEG)]
    return jnp.concatenate(outs, axis=1)


def _rwkv_kernel(has_vres, tb, *refs):
    if has_vres:
        (x_ref, vf_ref, g_ref, wa_ref, mu_ref, wl_ref, w0_ref, a0_ref, v0_ref, kk_ref, ka_ref, rk_ref,
         gng_ref, gnb_ref, wp_ref, cm_ref, cm16_ref, tri_ref, ones_ref,
         y_ref, *scratch) = refs
        vout_ref = None
    else:
        (x_ref, g_ref, wa_ref, mu_ref, wl_ref, w0_ref, a0_ref, kk_ref, ka_ref, rk_ref,
         gng_ref, gnb_ref, wp_ref, cm_ref, cm16_ref, tri_ref, ones_ref,
         y_ref, vout_ref, *scratch) = refs
        vf_ref = v0_ref = None
    (carry_ref, state_ref, at_ref, rt_ref, bt_ref, kt_ref, bh_ref, kh_ref, vv_ref, pc_ref, wkv_ref,
     rp_ref, yl_ref, mm_ref, gm_ref) = scratch
    lw_in = wl_ref.shape[0]
    sh = 3 * D_MODEL + lw_in

    @pl.when(pl.program_id(1) == 0)
    def _():
        carry_ref[...] = jnp.zeros_like(carry_ref)
        state_ref[...] = jnp.zeros_like(state_ref)

    x = x_ref[...]
    h = _rms_norm(x, g_ref[...]).astype(BF16)
    ps = jnp.dot(h, wa_ref[:, :sh], preferred_element_type=F32)
    gate = jnp.dot(h, wa_ref[:, sh:], preferred_element_type=F32)

    row = lax.broadcasted_iota(jnp.int32, (tb, 1), 0)
    prev = jnp.where(row == 0, carry_ref[0:1, :], pltpu.roll(ps, 1, 0))
    carry_ref[0:1, :] = ps[tb - 1:tb, :]
    shifted = ps + (prev - ps) * mu_ref[...]
    r = shifted[:, 0:D_MODEL]
    k = shifted[:, D_MODEL:2 * D_MODEL]
    v = shifted[:, 2 * D_MODEL:3 * D_MODEL]
    lo_in = shifted[:, 3 * D_MODEL:sh]
    lane = lax.broadcasted_iota(jnp.int32, lo_in.shape, 1)
    lo_act = jnp.where(lane < LORA_IN, jnp.tanh(lo_in), lo_in)
    lora = _mm(lo_act, wl_ref[...])

    logw = -math.exp(-0.5) * _sigmoid(w0_ref[...] + lora[:, 0:D_MODEL])
    a_icl = _sigmoid(a0_ref[...] + lora[:, D_MODEL:2 * D_MODEL])
    if has_vres:
        v = v + (vf_ref[...] - v) * _sigmoid(v0_ref[...] + lora[:, 2 * D_MODEL:3 * D_MODEL])
    else:
        vout_ref[...] = v
    ones_bd = ones_ref[...]
    kk = k * kk_ref[...]
    kk = kk / jnp.maximum(jnp.sqrt(_seg_sum(kk * kk, ones_bd)), 1e-12)
    k2 = k * (1.0 + (a_icl - 1.0) * ka_ref[...])
    b_vec = kk * a_icl

    tri = tri_ref[...]
    hi, lo = _split2(logw)
    cum = jnp.dot(tri, hi, preferred_element_type=F32) + jnp.dot(tri, lo, preferred_element_type=F32)
    n_chunk = tb // CHUNK
    tot_rows = [cum[(c + 1) * CHUNK - 1:(c + 1) * CHUNK, :] for c in range(n_chunk)]
    tot = jnp.concatenate([jnp.broadcast_to(t, (CHUNK, D_MODEL)) for t in tot_rows], axis=0)
    e_k = jnp.exp(-cum)
    e_h = jnp.exp(tot - cum)

    def to_pairs(ref, val):
        for p in range(N_PAIR):
            ref[p] = val[:, p * PAIR:(p + 1) * PAIR]

    to_pairs(at_ref, (-kk * jnp.exp(cum - logw)).astype(BF16))
    to_pairs(rt_ref, (r * jnp.exp(cum)).astype(BF16))
    to_pairs(bt_ref, (b_vec * e_k).astype(BF16))
    to_pairs(kt_ref, (k2 * e_k).astype(BF16))
    to_pairs(bh_ref, (b_vec * e_h).astype(BF16))
    to_pairs(kh_ref, (k2 * e_h).astype(BF16))
    to_pairs(vv_ref, v.astype(BF16))
    for c in range(n_chunk):
        p_c = jnp.exp(tot_rows[c])
        for p in range(N_PAIR):
            pc_ref[c, p] = jnp.broadcast_to(p_c[:, p * PAIR:(p + 1) * PAIR], (8, PAIR))

    def block_diag(t):
        t2 = jnp.concatenate([t, t], axis=1)
        same_head = (lax.broadcasted_iota(jnp.int32, (1,) + t2.shape[1:], 1) // HEAD
                     == (lax.broadcasted_iota(jnp.int32, (1,) + t2.shape[1:], 2) % PAIR) // HEAD)
        return jnp.where(same_head, t2, jnp.zeros_like(t2))

    ms_strict = cm_ref[MS_STRICT, 0:CHUNK, :]
    ms_incl = cm_ref[MS_INCL, 0:CHUNK, :]
    n_grp = CHUNK_GROUP * N_PAIR

    def group_body(g, carry):
        def load(ref):
            parts = [ref[:, pl.ds(pl.multiple_of((g * CHUNK_GROUP + j) * CHUNK, CHUNK), CHUNK), :]
                     for j in range(CHUNK_GROUP)]
            return jnp.concatenate(parts, axis=0)

        out = pl.ds(pl.multiple_of(g * n_grp, n_grp), n_grp)
        at, rt = load(at_ref), load(rt_ref)
        gmat = _bmm_nt(jnp.concatenate([at, rt], axis=1),
                       jnp.concatenate([block_diag(load(bt_ref)), block_diag(load(kt_ref))], axis=1))
        xab = gmat[:, 0:CHUNK, 0:PAIR] * ms_strict
        aak = gmat[:, 0:CHUNK, PAIR:2 * PAIR] * ms_strict
        arbk = jnp.concatenate([gmat[:, CHUNK:PAIR, 0:PAIR] * ms_incl,
                                gmat[:, CHUNK:PAIR, PAIR:2 * PAIR] * ms_incl], axis=2).astype(BF16)
        xab16 = xab.astype(BF16)
        xab2 = jnp.concatenate([xab16, xab16], axis=1)
        tinv = cm_ref[MS_EYE, 0:CHUNK, :] + xab * cm_ref[MS_BLK2, 0:CHUNK, :]
        for lvl in range(len(INVERSE_LEVELS)):
            t16 = tinv.astype(BF16)
            tinv = tinv + _bmm(_bmm(t16, xab2 * cm16_ref[M_OFF0 + lvl]), block_diag(t16))
        vbd = block_diag(load(vv_ref))
        zl = _bmm(aak, vbd).astype(BF16)
        aw = _bmm(tinv, jnp.concatenate([block_diag(at), block_diag(zl)], axis=2))
        q = jnp.concatenate([block_diag(aw.astype(BF16)),
                             jnp.concatenate([jnp.zeros_like(vbd), vbd], axis=2)], axis=1)
        ry = _bmm(arbk, q)
        rp_ref[out] = (rt.astype(F32) + ry[:, :, 0:PAIR]).astype(BF16)
        yl_ref[out] = ry[:, :, PAIR:2 * PAIR]
        mg = _bmm_tn(q, jnp.concatenate([block_diag(load(bh_ref)), block_diag(load(kh_ref))], axis=1))
        mm_ref[out] = mg[:, 0:PAIR].astype(BF16)
        gm_ref[out] = mg[:, PAIR:2 * PAIR]
        return carry

    lax.fori_loop(0, tb // (CHUNK * CHUNK_GROUP), group_body, 0)

    def chunk_body(c, carry):
        sel = pl.ds(pl.multiple_of(c * N_PAIR, N_PAIR), N_PAIR)
        s0 = state_ref[...]
        s16 = s0.astype(BF16)
        wkv_ref[:, pl.ds(pl.multiple_of(c * CHUNK, CHUNK), CHUNK), :] = _bmm_nt(rp_ref[sel], s16) + yl_ref[sel]
        state_ref[...] = s0 * pc_ref[c][:, 0:1, :] + _bmm(s16, mm_ref[sel]) + gm_ref[sel]
        return carry

    lax.fori_loop(0, n_chunk, chunk_body, 0)

    wkv = jnp.concatenate([wkv_ref[p] for p in range(N_PAIR)], axis=1)
    inv_n = 1.0 / HEAD
    mean = _seg_sum(wkv, ones_bd) * inv_n
    dev = wkv - mean
    var = _seg_sum(dev * dev, ones_bd) * inv_n
    wkv_n = dev * lax.rsqrt(var + GN_EPS) * gng_ref[...] + gnb_ref[...]
    bonus = _seg_sum(r * k2 * rk_ref[...], ones_bd) * v
    y_ref[...] = _mm((wkv_n + bonus) * _silu(gate), wp_ref[...])


def _const_spec(shape):
    nd = len(shape)
    return pl.BlockSpec(shape, lambda b, s: (0,) * nd, pipeline_mode=pl.Buffered(1))


def _rwkv_call(x, v_first, g, wa, mu, wl, w0, a0, v0, k_k, k_a, r_k, gn_g, gn_b, wp, cm, tri, ones_bd):
    bsz, seq, d = x.shape
    tb = min(TOKEN_BLOCK, seq)
    has_vres = v_first is not None
    blk = pl.BlockSpec((None, tb, d), lambda b, s: (b, s, 0))
    sh = mu.shape[1]
    consts = ([g, wa, mu, wl, w0, a0] + ([v0] if has_vres else [])
              + [k_k, k_a, r_k, gn_g, gn_b, wp, cm, cm.astype(BF16), tri, ones_bd])
    in_specs = [blk] + ([blk] if has_vres else []) + [_const_spec(c.shape) for c in consts]
    args = [x] + ([v_first] if has_vres else []) + consts
    out_sds = jax.ShapeDtypeStruct((bsz, seq, d), F32)
    big = lambda dt: pltpu.VMEM((N_PAIR, tb, PAIR), dt)
    n_chunk = tb // CHUNK
    mat = lambda r, c, dt: pltpu.VMEM((n_chunk * N_PAIR, r, c), dt)
    scratch = [pltpu.VMEM((8, sh), F32), pltpu.VMEM((N_PAIR, PAIR, PAIR), F32),
               big(BF16), big(BF16), big(BF16), big(BF16), big(BF16), big(BF16), big(BF16),
               pltpu.VMEM((n_chunk, N_PAIR, 8, PAIR), F32), big(F32),
               mat(CHUNK, PAIR, BF16), mat(CHUNK, PAIR, F32), mat(PAIR, PAIR, BF16), mat(PAIR, PAIR, F32)]
    return pl.pallas_call(
        functools.partial(_rwkv_kernel, has_vres, tb),
        grid=(bsz, seq // tb),
        in_specs=in_specs,
        out_specs=blk if has_vres else (blk, blk),
        out_shape=out_sds if has_vres else (out_sds, out_sds),
        scratch_shapes=scratch,
        compiler_params=pltpu.CompilerParams(dimension_semantics=("arbitrary", "arbitrary"),
                                             vmem_limit_bytes=VMEM_LIMIT),
        name="rwkv_vres" if has_vres else "rwkv",
    )(*args)


def _token_kernel(final_norm, tb, *refs):
    if final_norm:
        (x_ref, yr_ref, kv_ref, g_ref, wb_ref, bglu_ref, wdw_ref, bdw_ref, lng_ref, lnb_ref, wpc_ref, bpc_ref,
         wpm_ref, wout_ref, gfin_ref, out_ref, ubuf_ref, ush_ref, conv_ref, proj_ref, h_ref) = refs
    else:
        (x_ref, yr_ref, kv_ref, g_ref, wb_ref, bglu_ref, wdw_ref, bdw_ref, lng_ref, lnb_ref, wpc_ref, bpc_ref,
         wpm_ref, wout_ref, out_ref, ubuf_ref, ush_ref, conv_ref, proj_ref, h_ref) = refs
        gfin_ref = None
    d = D_MODEL

    @pl.when(pl.program_id(1) == 0)
    def _():
        ubuf_ref[0:CONV_HALO, :] = jnp.zeros((CONV_HALO, d), F32)

    x = x_ref[...]
    h = _rms_norm(x, g_ref[...]).astype(BF16)

    glu = jnp.dot(h, wb_ref[:, 0:2 * d], preferred_element_type=F32) + bglu_ref[...]
    ubuf_ref[CONV_HALO:CONV_HALO + tb, :] = glu[:, 0:d] * _sigmoid(glu[:, d:2 * d])
    h_ref[...] = h
    n_sh = tb + CONV_HALO - SUBLANE

    def shift_copy(b):
        ush_ref[b - 1] = ubuf_ref[b:b + n_sh, :]

    def conv_tile(lt, r0):
        lanes = slice(lt * LANE, (lt + 1) * LANE)
        acc = jnp.broadcast_to(bdw_ref[:, lanes], (CONV_ROWS, LANE))
        for j in range(CONV_K):
            a, b = divmod(CONV_HALO - (CONV_K - 1) + j, SUBLANE)
            rows = slice(r0 + a * SUBLANE, r0 + a * SUBLANE + CONV_ROWS)
            tap = ubuf_ref[rows, lanes] if b == 0 else ush_ref[b - 1, rows, lanes]
            acc = acc + tap * wdw_ref[j:j + 1, lanes]
        conv_ref[r0:r0 + CONV_ROWS, lanes] = acc

    def proj_slab(i):
        cols = slice(i * MXU_COLS, (i + 1) * MXU_COLS)
        proj_ref[:, cols] = jnp.dot(h_ref[...], wb_ref[:, 2 * d + cols.start:2 * d + cols.stop],
                                    preferred_element_type=F32)

    vec_steps = ([(SHIFT_COPY_COST, functools.partial(shift_copy, b)) for b in range(1, SUBLANE)]
                 + [(CONV_TILE_COST, functools.partial(conv_tile, lt, r0))
                    for lt in range(d // LANE) for r0 in range(0, tb, CONV_ROWS)])
    n_slab = (wb_ref.shape[1] - 2 * d) // MXU_COLS
    vec_total = sum(c for c, _ in vec_steps)

    @pl.when(pl.program_id(1) >= 0)
    def _():
        done, vec_done = 0, 0
        for cost, step in vec_steps:
            vec_done += cost
            while done * vec_total < vec_done * n_slab:
                proj_slab(done)
                done += 1
            step()
    ubuf_ref[0:CONV_HALO, :] = ubuf_ref[tb:tb + CONV_HALO, :]
    acc = conv_ref[...]
    um = jnp.mean(acc, axis=-1, keepdims=True)
    ud = acc - um
    uv = jnp.mean(ud * ud, axis=-1, keepdims=True)
    u = ud * lax.rsqrt(uv + LN_EPS) * lng_ref[...] + lnb_ref[...]
    cgate = proj_ref[:, 0:d]
    y_conv = _mm(_silu(u) * _silu(cgate), wpc_ref[...]) + bpc_ref[...]

    q = proj_ref[:, d:2 * d].astype(BF16)
    heads = []
    for hd in range(MEM_HEADS):
        sl = slice(hd * MEM_HEAD_DIM, (hd + 1) * MEM_HEAD_DIM)
        sc = lax.dot_general(q[:, sl], kv_ref[:, sl], (((1,), (1,)), ((), ())),
                             preferred_element_type=F32) * (MEM_HEAD_DIM ** -0.5)
        sc = jnp.exp(sc - jnp.max(sc, axis=-1, keepdims=True))
        prob = sc / jnp.sum(sc, axis=-1, keepdims=True)
        heads.append(jnp.dot(prob.astype(BF16), kv_ref[:, d + hd * MEM_HEAD_DIM:d + (hd + 1) * MEM_HEAD_DIM],
                             preferred_element_type=F32))
    att = jnp.concatenate(heads, axis=1)
    mgate = proj_ref[:, 2 * d:3 * d]
    y_mem = _mm(att * _silu(mgate), wpm_ref[...])

    y = _sigmoid(proj_ref[:, 3 * d:4 * d]) * yr_ref[...]
    y = y + _sigmoid(proj_ref[:, 4 * d:5 * d]) * y_conv
    y = y + _sigmoid(proj_ref[:, 5 * d:6 * d]) * y_mem
    out = x + _mm(y, wout_ref[...])
    if final_norm:
        out = _rms_norm(out, gfin_ref[...])
    out_ref[...] = out


def _token_call(x, y_rwkv, kv, g, wb, b_glu, w_dw, b_dw, ln_g, ln_b, wpc, bpc, wpm, wout, g_final):
    bsz, seq, d = x.shape
    tb = min(TOKEN_BLOCK, seq)
    final_norm = g_final is not None
    blk = pl.BlockSpec((None, tb, d), lambda b, s: (b, s, 0))
    kv_spec = pl.BlockSpec((None, N_MEM, 2 * d), lambda b, s: (b, 0, 0))
    consts = [g, wb, b_glu, w_dw, b_dw, ln_g, ln_b, wpc, bpc, wpm, wout] + ([g_final] if final_norm else [])
    return pl.pallas_call(
        functools.partial(_token_kernel, final_norm, tb),
        grid=(bsz, seq // tb),
        in_specs=[blk, blk, kv_spec] + [_const_spec(c.shape) for c in consts],
        out_specs=blk,
        out_shape=jax.ShapeDtypeStruct((bsz, seq, d), F32),
        scratch_shapes=[pltpu.VMEM((tb + CONV_HALO, d), F32),
                        pltpu.VMEM((SUBLANE - 1, tb + CONV_HALO - SUBLANE, d), F32),
                        pltpu.VMEM((tb, d), F32),
                        pltpu.VMEM((tb, wb.shape[1] - 2 * d), F32),
                        pltpu.VMEM((tb, d), BF16)],
        compiler_params=pltpu.CompilerParams(dimension_semantics=("arbitrary", "arbitrary"),
                                             vmem_limit_bytes=VMEM_LIMIT),
        name="token_final" if final_norm else "token",
    )(x, y_rwkv, kv, *consts)


def _memkv_kernel(mem_ref, g_ref, w_ref, kv_ref):
    m = _rms_norm(mem_ref[...], g_ref[...])
    kv_ref[...] = _mm(m, w_ref[...]).astype(BF16)


def _memkv_call(mem, g_mem, w_kv):
    bsz, n_mem, d = mem.shape
    depth = g_mem.shape[0]
    return pl.pallas_call(
        _memkv_kernel,
        grid=(depth, bsz),
        in_specs=[pl.BlockSpec((None, n_mem, d), lambda l, b: (b, 0, 0)),
                  pl.BlockSpec((None, 1, d), lambda l, b: (l, 0, 0)),
                  pl.BlockSpec((None, d, 2 * d), lambda l, b: (l, 0, 0))],
        out_specs=pl.BlockSpec((None, None, n_mem, 2 * d), lambda l, b: (l, b, 0, 0)),
        out_shape=jax.ShapeDtypeStruct((depth, bsz, n_mem, 2 * d), BF16),
        compiler_params=pltpu.CompilerParams(dimension_semantics=("arbitrary", "arbitrary"),
                                             vmem_limit_bytes=VMEM_LIMIT),
        name="mem_kv",
    )(mem, g_mem, w_kv)


def _pair_masks():
    i = np.arange(PAIR)
    row, col = i[:, None], i[None, :]
    same = (row // HEAD) == (col // HEAD)
    rl, cl = row % HEAD, col % HEAD
    top = row < HEAD
    masks = [top & (cl < row), top & (cl <= row), top & (cl == row), top & (cl // 2 == row // 2) & (cl < row)]
    for n in INVERSE_LEVELS:
        masks.append(same & (rl // (2 * n) == cl // (2 * n)) & ((rl // n) % 2 == 1) & ((cl // n) % 2 == 0))
    return jnp.asarray(np.stack(masks).astype(np.float32))


def _cumsum_matrix(tb):
    i = np.arange(tb)
    tri = ((i[:, None] // CHUNK) == (i[None, :] // CHUNK)) & (i[None, :] <= i[:, None])
    return jnp.asarray(tri.astype(np.float32)).astype(BF16)


def _segment_ones():
    i = np.arange(SEG)
    return jnp.asarray(((i[:, None] // HEAD) == (i[None, :] // HEAD)).astype(np.float32)).astype(BF16)


def kernel(x, mem, g_norm, w_in, mu_shift, w0, w_decay_up, a0, w_aaa_up, k_k, k_a, r_k, gn_g, gn_b, w_proj_rwkv, w_vres_down, mu_vres, v0, w_vres_up, b_glu, w_dw, b_dw, ln_g, ln_b, w_proj_conv, b_proj_conv, g_mem_norm, w_mem_kv, w_proj_mem, w_out, g_final):
    depth, d = g_norm.shape
    seq = x.shape[1]
    tb = min(TOKEN_BLOCK, seq)
    n_shift = 3 * d + 2 * LORA_IN
    row = lambda a: a.reshape(1, -1).astype(F32)
    cm, tri, ones_bd = _pair_masks(), _cumsum_matrix(tb), _segment_ones()
    kv = _memkv_call(mem, g_mem_norm.reshape(depth, 1, d), w_mem_kv.astype(BF16))

    v_first = None
    for i in range(depth):
        has_vres = i > 0
        lw_in = 2 * PAIR if has_vres else PAIR
        w_shift, w_rest = w_in[i][:, :n_shift], w_in[i][:, n_shift:]
        wl = jnp.zeros((lw_in, (3 if has_vres else 2) * d), F32)
        wl = wl.at[0:LORA_IN, 0:d].set(w_decay_up[i]).at[LORA_IN:2 * LORA_IN, d:2 * d].set(w_aaa_up[i])
        if has_vres:
            pad_w = jnp.zeros((d, PAIR - VRES_IN), F32)
            wa = jnp.concatenate([w_shift, w_vres_down[i - 1], pad_w, w_rest[:, :d]], axis=1)
            mu = jnp.concatenate([mu_shift[i], mu_vres[i - 1], jnp.zeros((PAIR - VRES_IN,), F32)])
            wl = wl.at[PAIR:PAIR + VRES_IN, 2 * d:3 * d].set(w_vres_up[i - 1])
        else:
            wa = jnp.concatenate([w_shift, w_rest[:, :d]], axis=1)
            mu = mu_shift[i]
        outs = _rwkv_call(x, v_first, row(g_norm[i]), wa.astype(BF16), row(mu), wl.astype(BF16), row(w0[i]),
                          row(a0[i]), row(v0[i - 1]) if has_vres else None, row(k_k[i]), row(k_a[i]), row(r_k[i]),
                          row(gn_g[i]), row(gn_b[i]), w_proj_rwkv[i].astype(BF16), cm, tri, ones_bd)
        if has_vres:
            y_rwkv = outs
        else:
            y_rwkv, v_first = outs
        w_dw_pad = jnp.concatenate([w_dw[i], jnp.zeros((CONV_HALO - CONV_K, d), F32)], axis=0)
        x = _token_call(x, y_rwkv, kv[i], row(g_norm[i]), w_rest[:, d:].astype(BF16), row(b_glu[i]), w_dw_pad,
                        row(b_dw[i]), row(ln_g[i]), row(ln_b[i]), w_proj_conv[i].astype(BF16), row(b_proj_conv[i]),
                        w_proj_mem[i].astype(BF16), w_out[i].astype(BF16),
                        row(g_final) if i == depth - 1 else None)
    return x
```

```python
import functools
import math

import numpy as np
import jax
import jax.numpy as jnp
from jax import lax
from jax.experimental import pallas as pl
from jax.experimental.pallas import tpu as pltpu

F32 = jnp.float32
BF16 = jnp.bfloat16

D_MODEL = 1024
N_MEM = 256
HEAD = 64
PAIR = 2 * HEAD
N_PAIR = D_MODEL // PAIR
CHUNK = 64
CHUNK_GROUP = 2
LORA_IN = 64
N_SHIFT = 3 * D_MODEL + 2 * LORA_IN
GATE_COL = N_SHIFT + 2 * HEAD
VRES_IN = 32
CONV_K = 31
CONV_HALO = 32
CONV_ROWS = 64
SUBLANE = 8
LANE = 128
MEM_HEADS = 4
MEM_HEAD_DIM = D_MODEL // MEM_HEADS
RMS_EPS = 1e-6
LN_EPS = 1e-5
GN_EPS = 64e-5
SEG = 256
TOKEN_BLOCK = 256
VMEM_LIMIT = 56 * 1024 * 1024
INVERSE_LEVELS = (2, 4, 8, 16, 32)
MS_STRICT, MS_INCL, MS_EYE, MS_BLK2, M_OFF0 = 0, 1, 2, 3, 4


def _mm(a, b):
    return jnp.dot(a.astype(BF16), b.astype(BF16), preferred_element_type=F32)


def _bmm(a, b):
    return lax.dot_general(a.astype(BF16), b.astype(BF16), (((2,), (1,)), ((0,), (0,))),
                           preferred_element_type=F32)


def _bmm_nt(a, b):
    return lax.dot_general(a.astype(BF16), b.astype(BF16), (((2,), (2,)), ((0,), (0,))),
                           preferred_element_type=F32)


def _bmm_tn(a, b):
    return lax.dot_general(a.astype(BF16), b.astype(BF16), (((1,), (1,)), ((0,), (0,))),
                           preferred_element_type=F32)


def _rms_norm(x, g):
    return x * lax.rsqrt(jnp.mean(x * x, axis=-1, keepdims=True) + RMS_EPS) * g


def _sigmoid(x):
    return 0.5 * jnp.tanh(0.5 * x) + 0.5


def _silu(x):
    return x * _sigmoid(x)


def _split2(x):
    hi = x.astype(BF16)
    return hi, (x - hi.astype(F32)).astype(BF16)


def _seg_sum(x, ones_bd):
    xb = x.astype(BF16)
    outs = [jnp.dot(xb[:, g * SEG:(g + 1) * SEG], ones_bd, preferred_element_type=F32)
            for g in range(D_MODEL // SEG)]
    return jnp.concatenate(outs, axis=1)


def _rwkv_kernel(has_vres, tb, *refs):
    if has_vres:
        (x_ref, vf_ref, win_ref, g_ref, mu_ref, wl_ref, w0_ref, a0_ref, v0_ref, kk_ref, ka_ref, rk_ref,
         gng_ref, gnb_ref, wp_ref, cm_ref, cm16_ref, tri_ref, ones_ref,
         y_ref, *scratch) = refs
        vout_ref = None
    else:
        (x_ref, win_ref, g_ref, mu_ref, wl_ref, w0_ref, a0_ref, kk_ref, ka_ref, rk_ref,
         gng_ref, gnb_ref, wp_ref, cm_ref, cm16_ref, tri_ref, ones_ref,
         y_ref, vout_ref, *scratch) = refs
        vf_ref = v0_ref = None
    (carry_ref, state_ref, at_ref, rt_ref, bt_ref, kt_ref, bh_ref, kh_ref, vv_ref, pc_ref, wkv_ref) = scratch
    lw_in = wl_ref.shape[0]
    sh = 3 * D_MODEL + lw_in

    @pl.when(pl.program_id(1) == 0)
    def _():
        carry_ref[...] = jnp.zeros_like(carry_ref)
        state_ref[...] = jnp.zeros_like(state_ref)

    x = x_ref[...]
    h = _rms_norm(x, g_ref[...]).astype(BF16)
    ps = jnp.dot(h, win_ref[:, 0:sh], preferred_element_type=F32)
    gate = jnp.dot(h, win_ref[:, GATE_COL:GATE_COL + D_MODEL], preferred_element_type=F32)

    row = lax.broadcasted_iota(jnp.int32, (tb, 1), 0)
    prev = jnp.where(row == 0, carry_ref[0:1, :], pltpu.roll(ps, 1, 0))
    carry_ref[0:1, :] = ps[tb - 1:tb, :]
    shifted = ps + (prev - ps) * mu_ref[...]
    r = shifted[:, 0:D_MODEL]
    k = shifted[:, D_MODEL:2 * D_MODEL]
    v = shifted[:, 2 * D_MODEL:3 * D_MODEL]
    lo_in = shifted[:, 3 * D_MODEL:sh]
    lane = lax.broadcasted_iota(jnp.int32, lo_in.shape, 1)
    lo_act = jnp.where(lane < LORA_IN, jnp.tanh(lo_in), lo_in)
    lora = _mm(lo_act, wl_ref[...])

    logw = -math.exp(-0.5) * _sigmoid(w0_ref[...] + lora[:, 0:D_MODEL])
    a_icl = _sigmoid(a0_ref[...] + lora[:, D_MODEL:2 * D_MODEL])
    if has_vres:
        v = v + (vf_ref[...] - v) * _sigmoid(v0_ref[...] + lora[:, 2 * D_MODEL:3 * D_MODEL])
    else:
        vout_ref[...] = v
    ones_bd = ones_ref[...]
    kk = k * kk_ref[...]
    kk = kk * lax.rsqrt(jnp.maximum(_seg_sum(kk * kk, ones_bd), 1e-24))
    k2 = k * (1.0 + (a_icl - 1.0) * ka_ref[...])
    b_vec = kk * a_icl

    tri = tri_ref[...]
    hi, lo = _split2(logw)
    cum = jnp.dot(tri, hi, preferred_element_type=F32) + jnp.dot(tri, lo, preferred_element_type=F32)
    n_chunk = tb // CHUNK
    tot_rows = [cum[(c + 1) * CHUNK - 1:(c + 1) * CHUNK, :] for c in range(n_chunk)]
    tot = jnp.concatenate([jnp.broadcast_to(t, (CHUNK, D_MODEL)) for t in tot_rows], axis=0)
    e_k = jnp.exp(-cum)
    e_h = jnp.exp(tot - cum)

    def to_pairs(ref, val):
        for p in range(N_PAIR):
            ref[p] = val[:, p * PAIR:(p + 1) * PAIR]

    to_pairs(at_ref, (-kk * jnp.exp(cum - logw)).astype(BF16))
    to_pairs(rt_ref, (r * jnp.exp(cum)).astype(BF16))
    to_pairs(bt_ref, (b_vec * e_k).astype(BF16))
    to_pairs(kt_ref, (k2 * e_k).astype(BF16))
    to_pairs(bh_ref, (b_vec * e_h).astype(BF16))
    to_pairs(kh_ref, (k2 * e_h).astype(BF16))
    to_pairs(vv_ref, v.astype(BF16))
    for c in range(n_chunk):
        p_c = jnp.exp(tot_rows[c])
        for p in range(N_PAIR):
            pc_ref[c, p] = jnp.broadcast_to(p_c[:, p * PAIR:(p + 1) * PAIR], (8, PAIR))

    def block_diag(t):
        t2 = jnp.concatenate([t, t], axis=1)
        same_head = (lax.broadcasted_iota(jnp.int32, (1,) + t2.shape[1:], 1) // HEAD
                     == (lax.broadcasted_iota(jnp.int32, (1,) + t2.shape[1:], 2) % PAIR) // HEAD)
        return jnp.where(same_head, t2, jnp.zeros_like(t2))

    ms_strict = cm_ref[MS_STRICT, 0:CHUNK, :]
    ms_incl = cm_ref[MS_INCL, 0:CHUNK, :]
    def group_matrices(g, out):
        def load(ref):
            parts = [ref[:, (g * CHUNK_GROUP + j) * CHUNK:(g * CHUNK_GROUP + j + 1) * CHUNK, :]
                     for j in range(CHUNK_GROUP)]
            return jnp.concatenate(parts, axis=0)

        lhs = jnp.concatenate([load(at_ref), load(rt_ref)], axis=1)
        gmat = _bmm_nt(lhs, jnp.concatenate([block_diag(load(bt_ref)), block_diag(load(kt_ref))], axis=1))
        yield
        xab = gmat[:, 0:CHUNK, 0:PAIR] * ms_strict
        aak = gmat[:, 0:CHUNK, PAIR:2 * PAIR] * ms_strict
        arbk = jnp.concatenate([gmat[:, CHUNK:PAIR, 0:PAIR] * ms_incl,
                                gmat[:, CHUNK:PAIR, PAIR:2 * PAIR] * ms_incl], axis=2).astype(BF16)
        xab16 = xab.astype(BF16)
        xab2 = jnp.concatenate([xab16, xab16], axis=1)
        tinv = cm_ref[MS_EYE, 0:CHUNK, :] + xab * cm_ref[MS_BLK2, 0:CHUNK, :]
        for lvl in range(len(INVERSE_LEVELS)):
            t16 = tinv.astype(BF16)
            half = _bmm(t16, xab2 * cm16_ref[M_OFF0 + lvl])
            yield
            tinv = tinv + _bmm(half, block_diag(t16))
            yield
        vbd = block_diag(load(vv_ref))
        zl = _bmm(aak, vbd)
        bk = jnp.concatenate([block_diag(load(bh_ref)), block_diag(load(kh_ref))], axis=1)
        out.update(lhs=lhs, tinv=tinv.astype(BF16), zl=zl, arbk=arbk, vbd=vbd, bk=bk)
        yield

    state = [state_ref[...]]

    def group_recurrence(g, m):
        for j in range(CHUNK_GROUP):
            c = g * CHUNK_GROUP + j
            sel = slice(j * N_PAIR, (j + 1) * N_PAIR)
            s0 = state[0]
            ar = _bmm_nt(m["lhs"][sel], s0)
            yield
            u = _bmm(m["tinv"][sel], block_diag((ar[:, 0:CHUNK] + m["zl"][sel]).astype(BF16)))
            yield
            uv = jnp.concatenate([block_diag(u.astype(BF16)), m["vbd"][sel]], axis=1)
            state[0] = s0 * pc_ref[c][:, 0:1, :] + _bmm_tn(uv, m["bk"][sel])
            yield
            wkv_ref[:, c * CHUNK:(c + 1) * CHUNK, :] = ar[:, CHUNK:PAIR] + _bmm(m["arbk"][sel], uv)
            yield

    n_group = n_chunk // CHUNK_GROUP
    mats = [dict() for _ in range(n_group)]
    for g in range(n_group + 1):
        build = group_matrices(g, mats[g]) if g < n_group else iter(())
        chain = group_recurrence(g - 1, mats[g - 1]) if g > 0 else iter(())
        for _ in build:
            next(chain, None)
        for _ in chain:
            pass
    state_ref[...] = state[0]

    wkv = jnp.concatenate([wkv_ref[p] for p in range(N_PAIR)], axis=1)
    inv_n = 1.0 / HEAD
    mean = _seg_sum(wkv, ones_bd) * inv_n
    dev = wkv - mean
    var = _seg_sum(dev * dev, ones_bd) * inv_n
    wkv_n = dev * lax.rsqrt(var + GN_EPS) * gng_ref[...] + gnb_ref[...]
    bonus = _seg_sum(r * k2 * rk_ref[...], ones_bd) * v
    y_ref[...] = _mm((wkv_n + bonus) * _silu(gate), wp_ref[...])


def _const_spec(shape):
    nd = len(shape)
    return pl.BlockSpec(shape, lambda b, s: (0,) * nd, pipeline_mode=pl.Buffered(1))


def _layer_weight_spec(w_all, layer):
    return pl.BlockSpec((None,) + w_all.shape[1:], lambda b, s: (layer, 0, 0), pipeline_mode=pl.Buffered(1))


def _rwkv_call(x, v_first, w_all, layer, g, mu, wl, w0, a0, v0, k_k, k_a, r_k, gn_g, gn_b, wp, cm, tri, ones_bd):
    bsz, seq, d = x.shape
    tb = min(TOKEN_BLOCK, seq)
    has_vres = v_first is not None
    blk = pl.BlockSpec((None, tb, d), lambda b, s: (b, s, 0))
    sh = mu.shape[1]
    consts = ([g, mu, wl, w0, a0] + ([v0] if has_vres else [])
              + [k_k, k_a, r_k, gn_g, gn_b, wp, cm, cm.astype(BF16), tri, ones_bd])
    in_specs = ([blk] + ([blk] if has_vres else []) + [_layer_weight_spec(w_all, layer)]
                + [_const_spec(c.shape) for c in consts])
    args = [x] + ([v_first] if has_vres else []) + [w_all] + consts
    out_sds = jax.ShapeDtypeStruct((bsz, seq, d), F32)
    big = lambda dt: pltpu.VMEM((N_PAIR, tb, PAIR), dt)
    scratch = [pltpu.VMEM((8, sh), F32), pltpu.VMEM((N_PAIR, PAIR, PAIR), F32),
               big(BF16), big(BF16), big(BF16), big(BF16), big(BF16), big(BF16), big(BF16),
               pltpu.VMEM((tb // CHUNK, N_PAIR, 8, PAIR), F32), big(F32)]
    return pl.pallas_call(
        functools.partial(_rwkv_kernel, has_vres, tb),
        grid=(bsz, seq // tb),
        in_specs=in_specs,
        out_specs=blk if has_vres else (blk, blk),
        out_shape=out_sds if has_vres else (out_sds, out_sds),
        scratch_shapes=scratch,
        compiler_params=pltpu.CompilerParams(dimension_semantics=("arbitrary", "arbitrary"),
                                             vmem_limit_bytes=VMEM_LIMIT),
        name="rwkv_vres" if has_vres else "rwkv",
    )(*args)


def _token_kernel(final_norm, tb, *refs):
    if final_norm:
        (x_ref, yr_ref, kv_ref, win_ref, g_ref, bglu_ref, wdw_ref, bdw_ref, lng_ref, lnb_ref, wpc_ref, bpc_ref,
         wpm_ref, wout_ref, gfin_ref, out_ref, ubuf_ref, ush_ref, conv_ref) = refs
    else:
        (x_ref, yr_ref, kv_ref, win_ref, g_ref, bglu_ref, wdw_ref, bdw_ref, lng_ref, lnb_ref, wpc_ref, bpc_ref,
         wpm_ref, wout_ref, out_ref, ubuf_ref, ush_ref, conv_ref) = refs
        gfin_ref = None
    d = D_MODEL

    @pl.when(pl.program_id(1) == 0)
    def _():
        ubuf_ref[0:CONV_HALO, :] = jnp.zeros((CONV_HALO, d), F32)

    x = x_ref[...]
    h = _rms_norm(x, g_ref[...]).astype(BF16)

    c0 = GATE_COL + d
    glu = jnp.dot(h, win_ref[:, c0:c0 + 2 * d], preferred_element_type=F32) + bglu_ref[...]
    ubuf_ref[CONV_HALO:CONV_HALO + tb, :] = glu[:, 0:d] * _sigmoid(glu[:, d:2 * d])
    n_sh = tb + CONV_HALO - SUBLANE
    for b in range(1, SUBLANE):
        ush_ref[b - 1] = ubuf_ref[b:b + n_sh, :]

    for lt in range(d // LANE):
        lanes = slice(lt * LANE, (lt + 1) * LANE)
        for r0 in range(0, tb, CONV_ROWS):
            acc = jnp.broadcast_to(bdw_ref[:, lanes], (CONV_ROWS, LANE))
            for j in range(CONV_K):
                a, b = divmod(CONV_HALO - (CONV_K - 1) + j, SUBLANE)
                rows = slice(r0 + a * SUBLANE, r0 + a * SUBLANE + CONV_ROWS)
                tap = ubuf_ref[rows, lanes] if b == 0 else ush_ref[b - 1, rows, lanes]
                acc = acc + tap * wdw_ref[j:j + 1, lanes]
            conv_ref[r0:r0 + CONV_ROWS, lanes] = acc
    ubuf_ref[0:CONV_HALO, :] = ubuf_ref[tb:tb + CONV_HALO, :]
    acc = conv_ref[...]
    um = jnp.mean(acc, axis=-1, keepdims=True)
    ud = acc - um
    uv = jnp.mean(ud * ud, axis=-1, keepdims=True)
    u = ud * lax.rsqrt(uv + LN_EPS) * lng_ref[...] + lnb_ref[...]
    cgate = jnp.dot(h, win_ref[:, c0 + 2 * d:c0 + 3 * d], preferred_element_type=F32)
    y_conv = _mm(_silu(u) * _silu(cgate), wpc_ref[...]) + bpc_ref[...]

    q = jnp.dot(h, win_ref[:, c0 + 3 * d:c0 + 4 * d], preferred_element_type=F32).astype(BF16)
    heads = []
    for hd in range(MEM_HEADS):
        sl = slice(hd * MEM_HEAD_DIM, (hd + 1) * MEM_HEAD_DIM)
        sc = lax.dot_general(q[:, sl], kv_ref[:, sl], (((1,), (1,)), ((), ())),
                             preferred_element_type=F32) * (MEM_HEAD_DIM ** -0.5)
        sc = jnp.exp(sc - jnp.max(sc, axis=-1, keepdims=True))
        prob = sc * (1.0 / jnp.sum(sc, axis=-1, keepdims=True))
        heads.append(jnp.dot(prob.astype(BF16), kv_ref[:, d + hd * MEM_HEAD_DIM:d + (hd + 1) * MEM_HEAD_DIM],
                             preferred_element_type=F32))
    att = jnp.concatenate(heads, axis=1)
    mgate = jnp.dot(h, win_ref[:, c0 + 4 * d:c0 + 5 * d], preferred_element_type=F32)
    y_mem = _mm(att * _silu(mgate), wpm_ref[...])

    y = _sigmoid(jnp.dot(h, win_ref[:, c0 + 5 * d:c0 + 6 * d], preferred_element_type=F32)) * yr_ref[...]
    y = y + _sigmoid(jnp.dot(h, win_ref[:, c0 + 6 * d:c0 + 7 * d], preferred_element_type=F32)) * y_conv
    y = y + _sigmoid(jnp.dot(h, win_ref[:, c0 + 7 * d:c0 + 8 * d], preferred_element_type=F32)) * y_mem
    out = x + _mm(y, wout_ref[...])
    if final_norm:
        out = _rms_norm(out, gfin_ref[...])
    out_ref[...] = out


def _token_call(x, y_rwkv, kv, w_all, layer, g, b_glu, w_dw, b_dw, ln_g, ln_b, wpc, bpc, wpm, wout, g_final):
    bsz, seq, d = x.shape
    tb = min(TOKEN_BLOCK, seq)
    final_norm = g_final is not None
    blk = pl.BlockSpec((None, tb, d), lambda b, s: (b, s, 0))
    kv_spec = pl.BlockSpec((None, None, N_MEM, 2 * d), lambda b, s: (layer, b, 0, 0))
    consts = [g, b_glu, w_dw, b_dw, ln_g, ln_b, wpc, bpc, wpm, wout] + ([g_final] if final_norm else [])
    return pl.pallas_call(
        functools.partial(_token_kernel, final_norm, tb),
        grid=(bsz, seq // tb),
        in_specs=[blk, blk, kv_spec, _layer_weight_spec(w_all, layer)] + [_const_spec(c.shape) for c in consts],
        out_specs=blk,
        out_shape=jax.ShapeDtypeStruct((bsz, seq, d), F32),
        scratch_shapes=[pltpu.VMEM((tb + CONV_HALO, d), F32),
                        pltpu.VMEM((SUBLANE - 1, tb + CONV_HALO - SUBLANE, d), F32),
                        pltpu.VMEM((tb, d), F32)],
        compiler_params=pltpu.CompilerParams(dimension_semantics=("arbitrary", "arbitrary"),
                                             vmem_limit_bytes=VMEM_LIMIT),
        name="token_final" if final_norm else "token",
    )(x, y_rwkv, kv, w_all, *consts)


def _memkv_kernel(mem_ref, g_ref, w_ref, kv_ref):
    m = _rms_norm(mem_ref[...], g_ref[...])
    kv_ref[...] = _mm(m, w_ref[...]).astype(BF16)


def _memkv_call(mem, g_mem, w_kv):
    bsz, n_mem, d = mem.shape
    depth = g_mem.shape[0]
    return pl.pallas_call(
        _memkv_kernel,
        grid=(depth, bsz),
        in_specs=[pl.BlockSpec((None, n_mem, d), lambda l, b: (b, 0, 0)),
                  pl.BlockSpec((None, 1, d), lambda l, b: (l, 0, 0)),
                  pl.BlockSpec((None, d, 2 * d), lambda l, b: (l, 0, 0))],
        out_specs=pl.BlockSpec((None, None, n_mem, 2 * d), lambda l, b: (l, b, 0, 0)),
        out_shape=jax.ShapeDtypeStruct((depth, bsz, n_mem, 2 * d), BF16),
        compiler_params=pltpu.CompilerParams(dimension_semantics=("arbitrary", "arbitrary"),
                                             vmem_limit_bytes=VMEM_LIMIT),
        name="mem_kv",
    )(mem, g_mem, w_kv)


def _pair_masks():
    i = np.arange(PAIR)
    row, col = i[:, None], i[None, :]
    same = (row // HEAD) == (col // HEAD)
    rl, cl = row % HEAD, col % HEAD
    top = row < HEAD
    masks = [top & (cl < row), top & (cl <= row), top & (cl == row), top & (cl // 2 == row // 2) & (cl < row)]
    for n in INVERSE_LEVELS:
        masks.append(same & (rl // (2 * n) == cl // (2 * n)) & ((rl // n) % 2 == 1) & ((cl // n) % 2 == 0))
    return jnp.asarray(np.stack(masks).astype(np.float32))


def _cumsum_matrix(tb):
    i = np.arange(tb)
    tri = ((i[:, None] // CHUNK) == (i[None, :] // CHUNK)) & (i[None, :] <= i[:, None])
    return jnp.asarray(tri.astype(np.float32)).astype(BF16)


def _segment_ones():
    i = np.arange(SEG)
    return jnp.asarray(((i[:, None] // HEAD) == (i[None, :] // HEAD)).astype(np.float32)).astype(BF16)


def kernel(x, mem, g_norm, w_in, mu_shift, w0, w_decay_up, a0, w_aaa_up, k_k, k_a, r_k, gn_g, gn_b, w_proj_rwkv, w_vres_down, mu_vres, v0, w_vres_up, b_glu, w_dw, b_dw, ln_g, ln_b, w_proj_conv, b_proj_conv, g_mem_norm, w_mem_kv, w_proj_mem, w_out, g_final):
    depth, d = g_norm.shape
    seq = x.shape[1]
    tb = min(TOKEN_BLOCK, seq)
    row = lambda a: a.reshape(1, -1).astype(F32)
    cm, tri, ones_bd = _pair_masks(), _cumsum_matrix(tb), _segment_ones()
    kv = _memkv_call(mem, g_mem_norm.reshape(depth, 1, d), w_mem_kv.astype(BF16))
    w_vres = jnp.concatenate([jnp.zeros((1, d, VRES_IN), F32), w_vres_down], axis=0)
    w_vres = jnp.concatenate([w_vres, jnp.zeros((depth, d, PAIR - VRES_IN), F32)], axis=2)
    w_all = jnp.concatenate([w_in[:, :, :N_SHIFT], w_vres, w_in[:, :, N_SHIFT:]], axis=2).astype(BF16)

    v_first = None
    for i in range(depth):
        has_vres = i > 0
        lw_in = 2 * PAIR if has_vres else PAIR
        wl = jnp.zeros((lw_in, (3 if has_vres else 2) * d), F32)
        wl = wl.at[0:LORA_IN, 0:d].set(w_decay_up[i]).at[LORA_IN:2 * LORA_IN, d:2 * d].set(w_aaa_up[i])
        if has_vres:
            mu = jnp.concatenate([mu_shift[i], mu_vres[i - 1], jnp.zeros((PAIR - VRES_IN,), F32)])
            wl = wl.at[PAIR:PAIR + VRES_IN, 2 * d:3 * d].set(w_vres_up[i - 1])
        else:
            mu = mu_shift[i]
        outs = _rwkv_call(x, v_first, w_all, i, row(g_norm[i]), row(mu), wl.astype(BF16), row(w0[i]),
                          row(a0[i]), row(v0[i - 1]) if has_vres else None, row(k_k[i]), row(k_a[i]), row(r_k[i]),
                          row(gn_g[i]), row(gn_b[i]), w_proj_rwkv[i].astype(BF16), cm, tri, ones_bd)
        if has_vres:
            y_rwkv = outs
        else:
            y_rwkv, v_first = outs
        w_dw_pad = jnp.concatenate([w_dw[i], jnp.zeros((CONV_HALO - CONV_K, d), F32)], axis=0)
        x = _token_call(x, y_rwkv, kv, w_all, i, row(g_norm[i]), row(b_glu[i]), w_dw_pad,
                        row(b_dw[i]), row(ln_g[i]), row(ln_b[i]), w_proj_conv[i].astype(BF16), row(b_proj_conv[i]),
                        w_proj_mem[i].astype(BF16), w_out[i].astype(BF16),
                        row(g_final) if i == depth - 1 else None)
    return x
```
